```python
import jax, jax.numpy as jnp
from jax import lax
import numpy as np

D_MODEL = 2048
BATCH = 1
SEQ = 16384
DEPTH = 4

N_MIXERS = 3
N_POOL_LAYERS = (DEPTH + 2) // 3
N_SGU_LAYERS = (DEPTH + 1) // 3
N_ATTN_LAYERS = DEPTH // 3
RMS_EPS = 1e-6
D_FF = 4 * D_MODEL

POOL_WINDOWS = (2, 4, 8, 16)
POOL_N_GROUPS = len(POOL_WINDOWS)
POOL_GROUP_CH = D_MODEL // POOL_N_GROUPS

SGU_WIDTH = D_MODEL
SGU_CHUNK = 128
SGU_GROUPS = 8
SGU_GROUP_CH = SGU_WIDTH // SGU_GROUPS

ATTN_PATTERNS = ((128, 1), (512, 4), (2048, 16))
ATTN_GROUPS = len(ATTN_PATTERNS)
ATTN_HEADS = 8
HEAD_DIM = 128
ATTN_BLOCK = 128
ATTN_OUT_WIDTH = ATTN_HEADS * HEAD_DIM
ATTN_QKV_WIDTH = 3 * ATTN_GROUPS * ATTN_HEADS * HEAD_DIM
NEG_INF = -1e30

kernel_name = "hybrid_pool_sgu_dilated_attn_trunk"


def rmsnorm(x, gain):
    xf = x.astype(jnp.float32)
    y = xf * lax.rsqrt(jnp.mean(xf * xf, axis=-1, keepdims=True) + RMS_EPS)
    return (y * gain.astype(jnp.float32)).astype(x.dtype)


def pool_mixer(x, w_in, w_group, scale, w_out):
    B, S, _ = x.shape
    h = (x @ w_in).astype(jnp.float32).reshape(B, S, POOL_N_GROUPS, POOL_GROUP_CH)
    cs = jnp.cumsum(h, axis=1)
    pos = jnp.arange(S)
    outs = []
    for g, w in enumerate(POOL_WINDOWS):
        c = cs[:, :, g]
        c_prev = jnp.pad(c, ((0, 0), (w, 0), (0, 0)))[:, :S]
        count = jnp.minimum(pos + 1, w).astype(jnp.float32)[None, :, None]
        outs.append((c - c_prev) / count - h[:, :, g])
    pooled = jnp.stack(outs, axis=2)
    mixed = jnp.einsum('bsgc,gcd->bsgd', pooled, w_group.astype(jnp.float32))
    mixed = mixed.reshape(B, S, D_MODEL) * scale.astype(jnp.float32)
    return mixed.astype(x.dtype) @ w_out


def sgu_mixer(x, w_in, v_norm, w_s, b_s, w_out):
    B, S, _ = x.shape
    h = jax.nn.gelu(x @ w_in, approximate=False)
    u, v = jnp.split(h, 2, axis=-1)
    v = rmsnorm(v, v_norm)
    n_chunks = S // SGU_CHUNK
    vc = v.reshape(B, n_chunks, SGU_CHUNK, SGU_GROUPS, SGU_GROUP_CH)
    causal = jnp.tril(jnp.ones((SGU_CHUNK, SGU_CHUNK), dtype=bool))
    ws = jnp.where(causal[None], w_s, jnp.zeros_like(w_s))
    sp = jnp.einsum('gts,bnsgc->bntgc', ws, vc) + b_s.T[None, None, :, :, None]
    gated = u * sp.reshape(B, S, SGU_WIDTH)
    return gated @ w_out


def dilated_group_attention(q, k, v, window, dilation):
    B, S, H, Dh = q.shape
    n_keys = window // dilation + 1
    scale = HEAD_DIM ** -0.5
    k_pad = jnp.pad(k, ((0, 0), (window, 0), (0, 0), (0, 0)))
    v_pad = jnp.pad(v, ((0, 0), (window, 0), (0, 0), (0, 0)))
    qi = jnp.arange(ATTN_BLOCK)
    kj = jnp.arange(n_keys)
    offsets = qi[:, None] - dilation * kj[None, :]
    local_idx = window + offsets

    def block(blk):
        s0 = blk * ATTN_BLOCK
        qb = lax.dynamic_slice_in_dim(q, s0, ATTN_BLOCK, axis=1).astype(jnp.float32)
        kw = lax.dynamic_slice_in_dim(k_pad, s0, window + ATTN_BLOCK, axis=1)
        vw = lax.dynamic_slice_in_dim(v_pad, s0, window + ATTN_BLOCK, axis=1)
        kg = kw[:, local_idx].astype(jnp.float32)
        vg = vw[:, local_idx].astype(jnp.float32)
        s = jnp.einsum('bqhd,bqjhd->bhqj', qb, kg) * scale
        valid = (s0 + offsets) >= 0
        s = jnp.where(valid[None, None], s, jnp.float32(NEG_INF))
        m = jnp.max(s, axis=-1, keepdims=True)
        p = jnp.exp(s - m)
        den = jnp.sum(p, axis=-1, keepdims=True)
        o = jnp.einsum('bhqj,bqjhd->bqhd', p, vg)
        o = o / jnp.moveaxis(den[..., 0], 1, 2)[..., None]
        lse = jnp.moveaxis((m + jnp.log(den))[..., 0], 1, 2)
        return o, lse

    outs, lses = lax.map(block, jnp.arange(S // ATTN_BLOCK))
    o = jnp.moveaxis(outs, 0, 1).reshape(B, S, H, Dh)
    lse = jnp.moveaxis(lses, 0, 1).reshape(B, S, H)
    return o, lse


def attn_mixer(x, w_qkv, w_out):
    B, S, _ = x.shape
    qkv = (x @ w_qkv).reshape(B, S, 3, ATTN_GROUPS, ATTN_HEADS, HEAD_DIM)
    outs, lses = [], []
    for g, (window, dilation) in enumerate(ATTN_PATTERNS):
        o, lse = dilated_group_attention(qkv[:, :, 0, g], qkv[:, :, 1, g], qkv[:, :, 2, g],
                                         window, dilation)
        outs.append(o)
        lses.append(lse)
    weights = jax.nn.softmax(jnp.stack(lses, axis=0), axis=0)
    o = jnp.sum(weights[..., None] * jnp.stack(outs, axis=0), axis=0)
    return o.reshape(B, S, ATTN_OUT_WIDTH).astype(x.dtype) @ w_out


def squared_relu_mlp(x, w_up, w_down):
    h = jax.nn.relu(x @ w_up)
    return (h * h) @ w_down


def setup_inputs(seed: int = 0) -> dict:
    key = jax.random.key(seed)
    ks = jax.random.split(key, 20)
    f32 = jnp.float32

    def nrm(k, shape, fan_in):
        return jax.random.normal(k, shape, f32) * (fan_in ** -0.5)

    def gain(k, shape):
        return 1.0 + 0.05 * jax.random.normal(k, shape, f32)

    x = jax.random.normal(ks[0], (BATCH, SEQ, D_MODEL), f32)
    norm_mix = gain(ks[1], (DEPTH, D_MODEL))
    pool_w_in = nrm(ks[2], (N_POOL_LAYERS, D_MODEL, D_MODEL), D_MODEL)
    pool_w_group = nrm(ks[3], (N_POOL_LAYERS, POOL_N_GROUPS, POOL_GROUP_CH, POOL_GROUP_CH), POOL_GROUP_CH)
    pool_scale = gain(ks[4], (N_POOL_LAYERS, D_MODEL))
    pool_w_out = nrm(ks[5], (N_POOL_LAYERS, D_MODEL, D_MODEL), D_MODEL)
    sgu_w_in = nrm(ks[6], (N_SGU_LAYERS, D_MODEL, 2 * SGU_WIDTH), D_MODEL)
    sgu_v_norm = gain(ks[7], (N_SGU_LAYERS, SGU_WIDTH))
    sgu_w_s = nrm(ks[8], (N_SGU_LAYERS, SGU_GROUPS, SGU_CHUNK, SGU_CHUNK), SGU_CHUNK)
    sgu_b_s = 1.0 + 0.1 * jax.random.normal(ks[9], (N_SGU_LAYERS, SGU_GROUPS, SGU_CHUNK), f32)
    sgu_w_out = nrm(ks[10], (N_SGU_LAYERS, SGU_WIDTH, D_MODEL), SGU_WIDTH)
    attn_w_qkv = nrm(ks[11], (N_ATTN_LAYERS, D_MODEL, ATTN_QKV_WIDTH), D_MODEL)
    attn_w_out = nrm(ks[12], (N_ATTN_LAYERS, ATTN_OUT_WIDTH, D_MODEL), ATTN_OUT_WIDTH)
    norm_mlp = gain(ks[13], (DEPTH, D_MODEL))
    mlp_w_up = nrm(ks[14], (DEPTH, D_MODEL, D_FF), D_MODEL)
    mlp_w_down = nrm(ks[15], (DEPTH, D_FF, D_MODEL), D_FF)
    norm_final = gain(ks[16], (D_MODEL,))
    return {"x": x, "norm_mix": norm_mix,
            "pool_w_in": pool_w_in, "pool_w_group": pool_w_group,
            "pool_scale": pool_scale, "pool_w_out": pool_w_out,
            "sgu_w_in": sgu_w_in, "sgu_v_norm": sgu_v_norm, "sgu_w_s": sgu_w_s,
            "sgu_b_s": sgu_b_s, "sgu_w_out": sgu_w_out,
            "attn_w_qkv": attn_w_qkv, "attn_w_out": attn_w_out,
            "norm_mlp": norm_mlp, "mlp_w_up": mlp_w_up, "mlp_w_down": mlp_w_down,
            "norm_final": norm_final}


def reference(x, norm_mix, pool_w_in, pool_w_group, pool_scale, pool_w_out,
              sgu_w_in, sgu_v_norm, sgu_w_s, sgu_b_s, sgu_w_out,
              attn_w_qkv, attn_w_out, norm_mlp, mlp_w_up, mlp_w_down, norm_final):
    for i in range(DEPTH):
        kind = i % N_MIXERS
        j = i // N_MIXERS
        h = rmsnorm(x, norm_mix[i])
        if kind == 0:
            y = pool_mixer(h, pool_w_in[j], pool_w_group[j], pool_scale[j], pool_w_out[j])
        elif kind == 1:
            y = sgu_mixer(h, sgu_w_in[j], sgu_v_norm[j], sgu_w_s[j], sgu_b_s[j], sgu_w_out[j])
        else:
            y = attn_mixer(h, attn_w_qkv[j], attn_w_out[j])
        x = x + y.astype(x.dtype)
        h = rmsnorm(x, norm_mlp[i])
        x = x + squared_relu_mlp(h, mlp_w_up[i], mlp_w_down[i]).astype(x.dtype)
    return rmsnorm(x, norm_final)
```

```python
import functools

import jax
import jax.numpy as jnp
import numpy as np
from jax import lax
from jax.experimental import pallas as pl
from jax.experimental.pallas import tpu as pltpu

F32 = jnp.float32
BF16 = jnp.bfloat16

RMS_EPS = 1e-6
POOL_WINDOWS = (2, 4, 8, 16)
SGU_CHUNK = 128
SGU_GROUPS = 8
ATTN_PATTERNS = ((128, 1), (512, 4), (2048, 16))
ATTN_HEADS = 8
HEAD_DIM = 128
ATTN_BLOCK = 128
NEG_INF = -1e30

V7X_VMEM_LIMIT_BYTES = 60 * 1024 * 1024
LANES = 128

MLP_ROWS = 1024
MLP_FF_COLS = 512
MIX_ROWS = 512
SGU_ROWS = 256
PERM_ROWS = 256
PROJ_ROWS = 1024
ATTN_Q_ROWS = 512
POOL_HALO = 32


def _params(*semantics):
    return pltpu.CompilerParams(dimension_semantics=semantics,
                                vmem_limit_bytes=V7X_VMEM_LIMIT_BYTES)


def _resident(shape, index_map):
    return pl.BlockSpec(shape, index_map, pipeline_mode=pl.Buffered(1))


def _rmsnorm(xf, gain_row):
    ms = jnp.mean(xf * xf, axis=-1, keepdims=True)
    return (xf * lax.rsqrt(ms + RMS_EPS)) * gain_row


def _dot(a, b):
    return jnp.dot(a, b, preferred_element_type=F32)


def _mlp_kernel(x_ref, g_ref, wup_ref, wdn_ref, gf_ref, o_ref, xn_ref, *, final_norm):
    f = pl.program_id(1)

    @pl.when(f == 0)
    def _():
        x = x_ref[...]
        xn_ref[...] = _rmsnorm(x, g_ref[...]).astype(BF16)
        o_ref[...] = x

    h = jnp.maximum(_dot(xn_ref[...], wup_ref[...]), 0.0)
    o_ref[...] += _dot((h * h).astype(BF16), wdn_ref[...])

    if final_norm:
        @pl.when(f == pl.num_programs(1) - 1)
        def _():
            o_ref[...] = _rmsnorm(o_ref[...], gf_ref[...])


def _mlp(x, gain, w_up, w_down, final_gain, final_norm):
    s, d = x.shape
    ff = w_up.shape[1]
    tm, tf = MLP_ROWS, MLP_FF_COLS
    return pl.pallas_call(
        functools.partial(_mlp_kernel, final_norm=final_norm),
        grid=(s // tm, ff // tf),
        in_specs=[
            pl.BlockSpec((tm, d), lambda i, f: (i, 0)),
            pl.BlockSpec((1, d), lambda i, f: (0, 0)),
            pl.BlockSpec((d, tf), lambda i, f: (0, f)),
            pl.BlockSpec((tf, d), lambda i, f: (f, 0)),
            pl.BlockSpec((1, d), lambda i, f: (0, 0)),
        ],
        out_specs=pl.BlockSpec((tm, d), lambda i, f: (i, 0)),
        out_shape=jax.ShapeDtypeStruct((s, d), F32),
        scratch_shapes=[pltpu.VMEM((tm, d), BF16)],
        compiler_params=_params("parallel", "arbitrary"),
        name="mlp_final" if final_norm else "mlp",
    )(x, gain, w_up, w_down, final_gain)


def _pool_kernel(x_ref, g_ref, win_ref, wg_ref, sc_ref, wout_ref, o_ref,
                 hext_ref, t0_ref, t1_ref):
    b = pl.program_id(0)
    tm, d = x_ref.shape
    halo = POOL_HALO
    gc = d // len(POOL_WINDOWS)

    @pl.when(b == 0)
    def _():
        hext_ref[0:halo, :] = jnp.zeros((halo, d), F32)
        t0_ref[0:halo, :] = jnp.zeros((halo, gc), F32)
        t1_ref[0:halo, :] = jnp.zeros((halo, gc), F32)

    @pl.when(b > 0)
    def _():
        hext_ref[halo - 16:halo, :] = hext_ref[tm + halo - 16:tm + halo, :]

    x = x_ref[...]
    xn = _rmsnorm(x, g_ref[...]).astype(BF16)
    hext_ref[halo:halo + tm, :] = _dot(xn, win_ref[...])

    pos = b * tm + lax.broadcasted_iota(jnp.int32, (tm, 1), 0)
    temps = (t0_ref, t1_ref)
    mixed = []
    for g, w in enumerate(POOL_WINDOWS):
        cols = slice(g * gc, (g + 1) * gc)
        src, src_cols, shift, levels = hext_ref, cols, 1, g + 1
        for lvl in range(levels):
            last = lvl == levels - 1
            lo = halo if last else halo - 16
            n = tm if last else tm + 16
            s = src[lo:lo + n, src_cols] + src[lo - shift:lo - shift + n, src_cols]
            if not last:
                dst = temps[lvl % 2]
                dst[lo:lo + n, :] = s
                src, src_cols = dst, slice(None)
            shift *= 2
        inv_count = 1.0 / jnp.minimum(pos + 1, w).astype(F32)
        pooled = s * inv_count - hext_ref[halo:halo + tm, cols]
        mixed.append(_dot(pooled.astype(BF16), wg_ref[g]))
    mixed = (jnp.concatenate(mixed, axis=1) * sc_ref[...]).astype(BF16)
    o_ref[...] = x + _dot(mixed, wout_ref[...])


def _pool_mixer(x, gain, w_in, w_group, scale, w_out):
    s, d = x.shape
    tm = MIX_ROWS
    ng, gc, _ = w_group.shape
    return pl.pallas_call(
        _pool_kernel,
        grid=(s // tm,),
        in_specs=[
            pl.BlockSpec((tm, d), lambda i: (i, 0)),
            _resident((1, d), lambda i: (0, 0)),
            _resident((d, d), lambda i: (0, 0)),
            _resident((ng, gc, gc), lambda i: (0, 0, 0)),
            _resident((1, d), lambda i: (0, 0)),
            _resident((d, d), lambda i: (0, 0)),
        ],
        out_specs=pl.BlockSpec((tm, d), lambda i: (i, 0)),
        out_shape=jax.ShapeDtypeStruct((s, d), F32),
        scratch_shapes=[pltpu.VMEM((tm + POOL_HALO, d), F32),
                        pltpu.VMEM((tm + POOL_HALO, gc), F32),
                        pltpu.VMEM((tm + POOL_HALO, gc), F32)],
        compiler_params=_params("arbitrary"),
        name="pool_mixer",
    )(x, gain, w_in, w_group, scale, w_out)


def _gelu(x):
    return 0.5 * x * (1.0 + lax.erf(x * np.float32(np.sqrt(0.5))))


def _sgu_kernel(x_ref, g_ref, win_ref, vn_ref, ws_ref, bs_ref, wout_ref, o_ref,
                u_ref, v_ref, gated_ref):
    tm, d = x_ref.shape
    e = win_ref.shape[1] // 2
    gcols = e // SGU_GROUPS
    x = x_ref[...]
    xn = _rmsnorm(x, g_ref[...]).astype(BF16)
    u_ref[...] = _gelu(_dot(xn, win_ref[:, :e]))
    v = _gelu(_dot(xn, win_ref[:, e:]))
    v_ref[...] = _rmsnorm(v, vn_ref[...]).astype(BF16)

    t = lax.broadcasted_iota(jnp.int32, (SGU_CHUNK, SGU_CHUNK), 0)
    sidx = lax.broadcasted_iota(jnp.int32, (SGU_CHUNK, SGU_CHUNK), 1)
    causal = sidx <= t
    for g in range(SGU_GROUPS):
        ws = jnp.where(causal, ws_ref[g], jnp.zeros_like(ws_ref[g]))
        bias = bs_ref[:, g:g + 1]
        cols = slice(g * gcols, (g + 1) * gcols)
        for c in range(tm // SGU_CHUNK):
            rows = slice(c * SGU_CHUNK, (c + 1) * SGU_CHUNK)
            sp = _dot(ws, v_ref[rows, cols]) + bias
            gated_ref[rows, cols] = (u_ref[rows, cols] * sp).astype(BF16)
    o_ref[...] = x + _dot(gated_ref[...], wout_ref[...])


def _sgu_mixer(x, gain, w_in, v_norm, w_s, b_s_t, w_out):
    s, d = x.shape
    tm = SGU_ROWS
    e = w_out.shape[0]
    return pl.pallas_call(
        _sgu_kernel,
        grid=(s // tm,),
        in_specs=[
            pl.BlockSpec((tm, d), lambda i: (i, 0)),
            _resident((1, d), lambda i: (0, 0)),
            _resident((d, 2 * e), lambda i: (0, 0)),
            _resident((1, e), lambda i: (0, 0)),
            _resident(w_s.shape, lambda i: (0, 0, 0)),
            _resident(b_s_t.shape, lambda i: (0, 0)),
            _resident((e, d), lambda i: (0, 0)),
        ],
        out_specs=pl.BlockSpec((tm, d), lambda i: (i, 0)),
        out_shape=jax.ShapeDtypeStruct((s, d), F32),
        scratch_shapes=[pltpu.VMEM((tm, e), F32),
                        pltpu.VMEM((tm, e), BF16),
                        pltpu.VMEM((tm, e), BF16)],
        compiler_params=_params("parallel"),
        name="sgu_mixer",
    )(x, gain, w_in, v_norm, w_s, b_s_t, w_out)


def _residue_permutation(rows, dilation):
    n = np.arange(rows)
    per = rows // dilation
    old = dilation * (n % per) + n // per
    p = np.zeros((rows, rows), np.float32)
    p[n, old] = 1.0
    return p


def _norm_perm_kernel(x_ref, g_ref, p4_ref, p16_ref, o1_ref, o4_ref, o16_ref):
    xn = _rmsnorm(x_ref[...], g_ref[...]).astype(BF16)
    o1_ref[...] = xn
    for p_ref, o_ref in ((p4_ref, o4_ref), (p16_ref, o16_ref)):
        y = _dot(p_ref[...], xn).astype(BF16)
        o_ref[...] = y.reshape(o_ref.shape)


def _norm_perm(x, gain, perms):
    s, d = x.shape
    t = PERM_ROWS
    d4, d16 = ATTN_PATTERNS[1][1], ATTN_PATTERNS[2][1]
    return pl.pallas_call(
        _norm_perm_kernel,
        grid=(s // t,),
        in_specs=[
            pl.BlockSpec((t, d), lambda i: (i, 0)),
            _resident((1, d), lambda i: (0, 0)),
            _resident((t, t), lambda i: (0, 0)),
            _resident((t, t), lambda i: (0, 0)),
        ],
        out_specs=[
            pl.BlockSpec((t, d), lambda i: (i, 0)),
            pl.BlockSpec((d4, t // d4, d), lambda i: (0, i, 0)),
            pl.BlockSpec((d16, t // d16, d), lambda i: (0, i, 0)),
        ],
        out_shape=[
            jax.ShapeDtypeStruct((s, d), BF16),
            jax.ShapeDtypeStruct((d4, s // d4, d), BF16),
            jax.ShapeDtypeStruct((d16, s // d16, d), BF16),
        ],
        compiler_params=_params("parallel"),
        name="attn_norm_perm",
    )(x, gain, perms[0], perms[1])


def _proj_kernel(a_ref, w_ref, o_ref):
    o_ref[...] = _dot(a_ref[...], w_ref[...]).astype(o_ref.dtype)


def _qkv_proj(a, w_qkv, group):
    s, d = a.shape
    n = ATTN_HEADS * HEAD_DIM
    n_groups = len(ATTN_PATTERNS)
    tm = PROJ_ROWS
    return pl.pallas_call(
        _proj_kernel,
        grid=(3, s // tm),
        in_specs=[
            pl.BlockSpec((tm, d), lambda j, i: (i, 0)),
            pl.BlockSpec((d, n), lambda j, i: (0, j * n_groups + group)),
        ],
        out_specs=pl.BlockSpec((tm, n), lambda j, i: (i, j)),
        out_shape=jax.ShapeDtypeStruct((s, 3 * n), BF16),
        compiler_params=_params("parallel", "parallel"),
        name=f"qkv_proj_g{group}",
    )(a, w_qkv)


def _attn_kernel(q_ref, kp_ref, kc_ref, vp_ref, vc_ref, o_ref, lse_ref):
    ib = pl.program_id(1)
    tq = q_ref.shape[1]
    blk = ATTN_BLOCK
    scale = HEAD_DIM ** -0.5
    row = lax.broadcasted_iota(jnp.int32, (blk, blk), 0)
    col = lax.broadcasted_iota(jnp.int32, (blk, blk), 1)
    prev_ok = col >= row
    cur_ok = col <= row
    lane = lax.broadcasted_iota(jnp.int32, (blk, LANES), 1)
    contract_last = (((1,), (1,)), ((), ()))

    for sub in range(tq // blk):
        rows = slice(sub * blk, (sub + 1) * blk)
        lse_rows = jnp.zeros((blk, LANES), F32)
        for h in range(ATTN_HEADS):
            hc = slice(h * HEAD_DIM, (h + 1) * HEAD_DIM)
            q = q_ref[0, rows, hc]
            if sub == 0:
                k_prev, v_prev = kp_ref[0, :, hc], vp_ref[0, :, hc]
                p_ok = jnp.logical_and(prev_ok, ib > 0)
            else:
                prows = slice((sub - 1) * blk, sub * blk)
                k_prev, v_prev = kc_ref[0, prows, hc], vc_ref[0, prows, hc]
                p_ok = prev_ok
            k_cur, v_cur = kc_ref[0, rows, hc], vc_ref[0, rows, hc]
            s_p = lax.dot_general(q, k_prev, contract_last, preferred_element_type=F32) * scale
            s_c = lax.dot_general(q, k_cur, contract_last, preferred_element_type=F32) * scale
            s_p = jnp.where(p_ok, s_p, NEG_INF)
            s_c = jnp.where(cur_ok, s_c, NEG_INF)
            m = jnp.maximum(jnp.max(s_p, axis=-1, keepdims=True),
                            jnp.max(s_c, axis=-1, keepdims=True))
            p_p = jnp.exp(s_p - m)
            p_c = jnp.exp(s_c - m)
            den = jnp.sum(p_p, axis=-1, keepdims=True) + jnp.sum(p_c, axis=-1, keepdims=True)
            o = _dot(p_p.astype(BF16), v_prev) + _dot(p_c.astype(BF16), v_cur)
            o_ref[0, rows, hc] = (o / den).astype(o_ref.dtype)
            lse_rows = jnp.where(lane == h, m + jnp.log(den), lse_rows)
        lse_ref[0, rows, :] = lse_rows


def _attention(qkv, dilation):
    d, length, n3 = qkv.shape
    n = n3 // 3
    tq = ATTN_Q_ROWS
    sub_per_tile = tq // ATTN_BLOCK
    prev_map = lambda col: (lambda r, i: (r, jnp.maximum(i * sub_per_tile - 1, 0), col))
    cur_map = lambda col: (lambda r, i: (r, i, col))
    return pl.pallas_call(
        _attn_kernel,
        grid=(d, length // tq),
        in_specs=[
            pl.BlockSpec((1, tq, n), cur_map(0)),
            pl.BlockSpec((1, ATTN_BLOCK, n), prev_map(1)),
            pl.BlockSpec((1, tq, n), cur_map(1)),
            pl.BlockSpec((1, ATTN_BLOCK, n), prev_map(2)),
            pl.BlockSpec((1, tq, n), cur_map(2)),
        ],
        out_specs=[
            pl.BlockSpec((1, tq, n), cur_map(0)),
            pl.BlockSpec((1, tq, LANES), cur_map(0)),
        ],
        out_shape=[
            jax.ShapeDtypeStruct((d, length, n), BF16),
            jax.ShapeDtypeStruct((d, length, LANES), F32),
        ],
        compiler_params=_params("parallel", "parallel"),
        name=f"window_attn_d{dilation}",
    )(qkv, qkv, qkv, qkv, qkv)


def _split3(a):
    hi = a.astype(BF16)
    r1 = a - hi.astype(F32)
    mid = r1.astype(BF16)
    lo = (r1 - mid.astype(F32)).astype(BF16)
    return hi, mid, lo


def _merge_kernel(x_ref, o1_ref, o4_ref, o16_ref, l1_ref, l4_ref, l16_ref,
                  p4t_ref, p16t_ref, wout_ref, out_ref):
    t, n = o1_ref.shape
    outs = [o1_ref[...].astype(F32)]
    lses = [l1_ref[...]]
    for o_ref, l_ref, pt_ref in ((o4_ref, l4_ref, p4t_ref), (o16_ref, l16_ref, p16t_ref)):
        pt = pt_ref[...]
        outs.append(_dot(pt, o_ref[...].reshape(t, n)))
        hi, mid, lo = _split3(l_ref[...].reshape(t, LANES))
        lses.append((_dot(pt, hi) + _dot(pt, mid)) + _dot(pt, lo))
    m = jnp.maximum(jnp.maximum(lses[0], lses[1]), lses[2])
    es = [jnp.exp(l - m) for l in lses]
    inv = 1.0 / (es[0] + es[1] + es[2])
    ws = [e * inv for e in es]
    cols = []
    for h in range(ATTN_HEADS):
        hc = slice(h * HEAD_DIM, (h + 1) * HEAD_DIM)
        acc = ws[0][:, h:h + 1] * outs[0][:, hc]
        acc += ws[1][:, h:h + 1] * outs[1][:, hc]
        acc += ws[2][:, h:h + 1] * outs[2][:, hc]
        cols.append(acc)
    o = jnp.concatenate(cols, axis=1).astype(BF16)
    out_ref[...] = x_ref[...] + _dot(o, wout_ref[...])


def _attn_merge(x, outs, lses, perms_t, w_out):
    s, d = x.shape
    n = w_out.shape[0]
    t = PERM_ROWS
    d4, d16 = ATTN_PATTERNS[1][1], ATTN_PATTERNS[2][1]
    grouped = lambda dil, width: pl.BlockSpec((dil, t // dil, width), lambda i: (0, i, 0))
    return pl.pallas_call(
        _merge_kernel,
        grid=(s // t,),
        in_specs=[
            pl.BlockSpec((t, d), lambda i: (i, 0)),
            pl.BlockSpec((t, n), lambda i: (i, 0)),
            grouped(d4, n),
            grouped(d16, n),
            pl.BlockSpec((t, LANES), lambda i: (i, 0)),
            grouped(d4, LANES),
            grouped(d16, LANES),
            _resident((t, t), lambda i: (0, 0)),
            _resident((t, t), lambda i: (0, 0)),
            _resident((n, d), lambda i: (0, 0)),
        ],
        out_specs=pl.BlockSpec((t, d), lambda i: (i, 0)),
        out_shape=jax.ShapeDtypeStruct((s, d), F32),
        compiler_params=_params("parallel"),
        name="attn_merge",
    )(x, outs[0], outs[1], outs[2], lses[0], lses[1], lses[2],
      perms_t[0], perms_t[1], w_out)


def _attn_mixer(x, gain, w_qkv, w_out):
    s, d = x.shape
    n = ATTN_HEADS * HEAD_DIM
    perms = [_residue_permutation(PERM_ROWS, dil) for _, dil in ATTN_PATTERNS[1:]]
    p = [jnp.asarray(m, BF16) for m in perms]
    pt = [jnp.asarray(m.T, BF16) for m in perms]
    xn_groups = _norm_perm(x, gain, p)
    outs, lses = [], []
    for g, (_, dil) in enumerate(ATTN_PATTERNS):
        qkv = _qkv_proj(xn_groups[g].reshape(s, d), w_qkv, g)
        o, lse = _attention(qkv.reshape(dil, s // dil, 3 * n), dil)
        if dil == 1:
            o, lse = o.reshape(s, n), lse.reshape(s, LANES)
        outs.append(o)
        lses.append(lse)
    return _attn_merge(x, outs, lses, pt, w_out)


def kernel(x, norm_mix, pool_w_in, pool_w_group, pool_scale, pool_w_out, sgu_w_in, sgu_v_norm, sgu_w_s, sgu_b_s, sgu_w_out, attn_w_qkv, attn_w_out, norm_mlp, mlp_w_up, mlp_w_down, norm_final):
    batch, seq, d = x.shape
    depth = norm_mix.shape[0]
    n_mixers = 3
    bf = lambda w: w.astype(BF16)
    row = lambda v: v.reshape(1, -1)
    outs = []
    for bi in range(batch):
        h = x[bi]
        for i in range(depth):
            kind, j = i % n_mixers, i // n_mixers
            gain = row(norm_mix[i])
            if kind == 0:
                h = _pool_mixer(h, gain, bf(pool_w_in[j]), bf(pool_w_group[j]),
                                row(pool_scale[j]), bf(pool_w_out[j]))
            elif kind == 1:
                h = _sgu_mixer(h, gain, bf(sgu_w_in[j]), row(sgu_v_norm[j]), bf(sgu_w_s[j]),
                               sgu_b_s[j].T, bf(sgu_w_out[j]))
            else:
                h = _attn_mixer(h, gain, bf(attn_w_qkv[j]), bf(attn_w_out[j]))
            h = _mlp(h, row(norm_mlp[i]), bf(mlp_w_up[i]), bf(mlp_w_down[i]),
                     row(norm_final), final_norm=(i == depth - 1))
        outs.append(h)
    return jnp.stack(outs, axis=0)
```

```python
import functools

import jax
import jax.numpy as jnp
import numpy as np
from jax import lax
from jax.experimental import pallas as pl
from jax.experimental.pallas import tpu as pltpu

F32 = jnp.float32
BF16 = jnp.bfloat16

RMS_EPS = 1e-6
POOL_WINDOWS = (2, 4, 8, 16)
SGU_CHUNK = 128
SGU_GROUPS = 8
ATTN_PATTERNS = ((128, 1), (512, 4), (2048, 16))
ATTN_HEADS = 8
HEAD_DIM = 128
ATTN_BLOCK = 128
NEG_INF = -1e30

V7X_VMEM_LIMIT_BYTES = 60 * 1024 * 1024
LANES = 128

MLP_ROWS = 1024
MLP_FF_COLS = 512
MIX_ROWS = 512
SGU_ROWS = 256
PERM_ROWS = 256
PROJ_ROWS = 1024
ATTN_Q_ROWS = 512
POOL_HALO = 32


def _params(*semantics):
    return pltpu.CompilerParams(dimension_semantics=semantics,
                                vmem_limit_bytes=V7X_VMEM_LIMIT_BYTES)


def _resident(shape, index_map):
    return pl.BlockSpec(shape, index_map, pipeline_mode=pl.Buffered(1))


def _rmsnorm(xf, gain_row):
    ms = jnp.mean(xf * xf, axis=-1, keepdims=True)
    return (xf * lax.rsqrt(ms + RMS_EPS)) * gain_row


def _dot(a, b):
    return lax.dot_general(a, b, (((1,), (0,)), ((), ())), preferred_element_type=F32)


def _mlp_kernel(x_ref, g_ref, wup_ref, wdn_ref, gf_ref, o_ref, xn_ref, *, final_norm):
    f = pl.program_id(1)

    @pl.when(f == 0)
    def _():
        x = x_ref[...]
        xn_ref[...] = _rmsnorm(x, g_ref[...]).astype(BF16)
        o_ref[...] = x

    h = jnp.maximum(_dot(xn_ref[...], wup_ref[...]), 0.0)
    o_ref[...] += _dot((h * h).astype(BF16), wdn_ref[...])

    if final_norm:
        @pl.when(f == pl.num_programs(1) - 1)
        def _():
            o_ref[...] = _rmsnorm(o_ref[...], gf_ref[...])


def _mlp(x, gain, w_up, w_down, layer, final_gain, final_norm):
    s, d = x.shape
    ff = w_up.shape[2]
    tm, tf = MLP_ROWS, MLP_FF_COLS
    return pl.pallas_call(
        functools.partial(_mlp_kernel, final_norm=final_norm),
        grid=(s // tm, ff // tf),
        in_specs=[
            pl.BlockSpec((tm, d), lambda i, f: (i, 0)),
            pl.BlockSpec((1, d), lambda i, f: (0, 0)),
            pl.BlockSpec((None, d, tf), lambda i, f: (layer, 0, f)),
            pl.BlockSpec((None, tf, d), lambda i, f: (layer, f, 0)),
            pl.BlockSpec((1, d), lambda i, f: (0, 0)),
        ],
        out_specs=pl.BlockSpec((tm, d), lambda i, f: (i, 0)),
        out_shape=jax.ShapeDtypeStruct((s, d), F32),
        scratch_shapes=[pltpu.VMEM((tm, d), BF16)],
        compiler_params=_params("parallel", "arbitrary"),
        name="mlp_final" if final_norm else "mlp",
    )(x, gain, w_up, w_down, final_gain)


def _pool_kernel(x_ref, g_ref, win_ref, wg_ref, sc_ref, wout_ref, o_ref,
                 hext_ref, t0_ref, t1_ref):
    b = pl.program_id(0)
    tm, d = x_ref.shape
    halo = POOL_HALO
    gc = d // len(POOL_WINDOWS)

    @pl.when(b == 0)
    def _():
        hext_ref[0:halo, :] = jnp.zeros((halo, d), F32)
        t0_ref[0:halo, :] = jnp.zeros((halo, gc), F32)
        t1_ref[0:halo, :] = jnp.zeros((halo, gc), F32)

    @pl.when(b > 0)
    def _():
        hext_ref[halo - 16:halo, :] = hext_ref[tm + halo - 16:tm + halo, :]

    x = x_ref[...]
    xn = _rmsnorm(x, g_ref[...]).astype(BF16)
    hext_ref[halo:halo + tm, :] = _dot(xn, win_ref[...])

    pos = b * tm + lax.broadcasted_iota(jnp.int32, (tm, 1), 0)
    temps = (t0_ref, t1_ref)
    mixed = []
    for g, w in enumerate(POOL_WINDOWS):
        cols = slice(g * gc, (g + 1) * gc)
        src, src_cols, shift, levels = hext_ref, cols, 1, g + 1
        for lvl in range(levels):
            last = lvl == levels - 1
            lo = halo if last else halo - 16
            n = tm if last else tm + 16
            s = src[lo:lo + n, src_cols] + src[lo - shift:lo - shift + n, src_cols]
            if not last:
                dst = temps[lvl % 2]
                dst[lo:lo + n, :] = s
                src, src_cols = dst, slice(None)
            shift *= 2
        inv_count = 1.0 / jnp.minimum(pos + 1, w).astype(F32)
        pooled = s * inv_count - hext_ref[halo:halo + tm, cols]
        mixed.append(_dot(pooled.astype(BF16), wg_ref[g]))
    mixed = (jnp.concatenate(mixed, axis=1) * sc_ref[...]).astype(BF16)
    o_ref[...] = x + _dot(mixed, wout_ref[...])


def _pool_mixer(x, gain, w_in, w_group, scale, w_out):
    s, d = x.shape
    tm = MIX_ROWS
    ng, gc, _ = w_group.shape
    return pl.pallas_call(
        _pool_kernel,
        grid=(s // tm,),
        in_specs=[
            pl.BlockSpec((tm, d), lambda i: (i, 0)),
            _resident((1, d), lambda i: (0, 0)),
            _resident((d, d), lambda i: (0, 0)),
            _resident((ng, gc, gc), lambda i: (0, 0, 0)),
            _resident((1, d), lambda i: (0, 0)),
            _resident((d, d), lambda i: (0, 0)),
        ],
        out_specs=pl.BlockSpec((tm, d), lambda i: (i, 0)),
        out_shape=jax.ShapeDtypeStruct((s, d), F32),
        scratch_shapes=[pltpu.VMEM((tm + POOL_HALO, d), F32),
                        pltpu.VMEM((tm + POOL_HALO, gc), F32),
                        pltpu.VMEM((tm + POOL_HALO, gc), F32)],
        compiler_params=_params("arbitrary"),
        name="pool_mixer",
    )(x, gain, w_in, w_group, scale, w_out)


def _gelu(x):
    return 0.5 * x * (1.0 + lax.erf(x * np.float32(np.sqrt(0.5))))


def _sgu_kernel(x_ref, g_ref, win_ref, vn_ref, ws_ref, bs_ref, wout_ref, o_ref,
                u_ref, v_ref, gated_ref):
    tm, d = x_ref.shape
    e = win_ref.shape[1] // 2
    gcols = e // SGU_GROUPS
    x = x_ref[...]
    xn = _rmsnorm(x, g_ref[...]).astype(BF16)
    u_ref[...] = _gelu(_dot(xn, win_ref[:, :e]))
    v = _gelu(_dot(xn, win_ref[:, e:]))
    v_ref[...] = _rmsnorm(v, vn_ref[...]).astype(BF16)

    t = lax.broadcasted_iota(jnp.int32, (SGU_CHUNK, SGU_CHUNK), 0)
    sidx = lax.broadcasted_iota(jnp.int32, (SGU_CHUNK, SGU_CHUNK), 1)
    causal = sidx <= t
    for g in range(SGU_GROUPS):
        ws = jnp.where(causal, ws_ref[g], jnp.zeros_like(ws_ref[g]))
        bias = bs_ref[:, g:g + 1]
        cols = slice(g * gcols, (g + 1) * gcols)
        for c in range(tm // SGU_CHUNK):
            rows = slice(c * SGU_CHUNK, (c + 1) * SGU_CHUNK)
            sp = _dot(ws, v_ref[rows, cols]) + bias
            gated_ref[rows, cols] = (u_ref[rows, cols] * sp).astype(BF16)
    o_ref[...] = x + _dot(gated_ref[...], wout_ref[...])


def _sgu_mixer(x, gain, w_in, v_norm, w_s, b_s_t, w_out):
    s, d = x.shape
    tm = SGU_ROWS
    e = w_out.shape[0]
    return pl.pallas_call(
        _sgu_kernel,
        grid=(s // tm,),
        in_specs=[
            pl.BlockSpec((tm, d), lambda i: (i, 0)),
            _resident((1, d), lambda i: (0, 0)),
            _resident((d, 2 * e), lambda i: (0, 0)),
            _resident((1, e), lambda i: (0, 0)),
            _resident(w_s.shape, lambda i: (0, 0, 0)),
            _resident(b_s_t.shape, lambda i: (0, 0)),
            _resident((e, d), lambda i: (0, 0)),
        ],
        out_specs=pl.BlockSpec((tm, d), lambda i: (i, 0)),
        out_shape=jax.ShapeDtypeStruct((s, d), F32),
        scratch_shapes=[pltpu.VMEM((tm, e), F32),
                        pltpu.VMEM((tm, e), BF16),
                        pltpu.VMEM((tm, e), BF16)],
        compiler_params=_params("parallel"),
        name="sgu_mixer",
    )(x, gain, w_in, v_norm, w_s, b_s_t, w_out)


def _residue_permutation(rows, dilation):
    n = np.arange(rows)
    per = rows // dilation
    old = dilation * (n % per) + n // per
    p = np.zeros((rows, rows), np.float32)
    p[n, old] = 1.0
    return p


def _norm_perm_kernel(x_ref, g_ref, p4_ref, p16_ref, o1_ref, o4_ref, o16_ref):
    xn = _rmsnorm(x_ref[...], g_ref[...]).astype(BF16)
    o1_ref[...] = xn
    for p_ref, o_ref in ((p4_ref, o4_ref), (p16_ref, o16_ref)):
        y = _dot(p_ref[...], xn).astype(BF16)
        o_ref[...] = y.reshape(o_ref.shape)


def _norm_perm(x, gain, perms):
    s, d = x.shape
    t = PERM_ROWS
    d4, d16 = ATTN_PATTERNS[1][1], ATTN_PATTERNS[2][1]
    return pl.pallas_call(
        _norm_perm_kernel,
        grid=(s // t,),
        in_specs=[
            pl.BlockSpec((t, d), lambda i: (i, 0)),
            _resident((1, d), lambda i: (0, 0)),
            _resident((t, t), lambda i: (0, 0)),
            _resident((t, t), lambda i: (0, 0)),
        ],
        out_specs=[
            pl.BlockSpec((t, d), lambda i: (i, 0)),
            pl.BlockSpec((d4, t // d4, d), lambda i: (0, i, 0)),
            pl.BlockSpec((d16, t // d16, d), lambda i: (0, i, 0)),
        ],
        out_shape=[
            jax.ShapeDtypeStruct((s, d), BF16),
            jax.ShapeDtypeStruct((d4, s // d4, d), BF16),
            jax.ShapeDtypeStruct((d16, s // d16, d), BF16),
        ],
        compiler_params=_params("parallel"),
        name="attn_norm_perm",
    )(x, gain, perms[0], perms[1])


def _proj_kernel(a_ref, w_ref, o_ref):
    o_ref[...] = _dot(a_ref[...], w_ref[...]).astype(o_ref.dtype)


def _qkv_proj(a, w_qkv, layer, group):
    s, d = a.shape
    n = ATTN_HEADS * HEAD_DIM
    n_groups = len(ATTN_PATTERNS)
    tm = PROJ_ROWS
    return pl.pallas_call(
        _proj_kernel,
        grid=(3, s // tm),
        in_specs=[
            pl.BlockSpec((tm, d), lambda j, i: (i, 0)),
            pl.BlockSpec((None, d, n), lambda j, i: (layer, 0, j * n_groups + group)),
        ],
        out_specs=pl.BlockSpec((tm, n), lambda j, i: (i, j)),
        out_shape=jax.ShapeDtypeStruct((s, 3 * n), BF16),
        compiler_params=_params("parallel", "parallel"),
        name=f"qkv_proj_g{group}",
    )(a, w_qkv)


def _attn_kernel(q_ref, kp_ref, kc_ref, vp_ref, vc_ref, o_ref, lse_ref):
    ib = pl.program_id(1)
    tq = q_ref.shape[1]
    blk = ATTN_BLOCK
    scale = HEAD_DIM ** -0.5
    heads = range(ATTN_HEADS)
    row = lax.broadcasted_iota(jnp.int32, (blk, 2 * blk), 0)
    col = lax.broadcasted_iota(jnp.int32, (blk, 2 * blk), 1)
    band = jnp.logical_and(col >= row, col <= row + blk)
    first_band = jnp.logical_and(band, jnp.logical_or(col >= blk, ib > 0))
    lane = lax.broadcasted_iota(jnp.int32, (blk, LANES), 1)
    contract_last = (((1,), (1,)), ((), ()))

    def window(prev_ref, cur_ref, sub, hc):
        if sub == 0:
            return jnp.concatenate([prev_ref[0, :, hc], cur_ref[0, 0:blk, hc]], axis=0)
        return cur_ref[0, (sub - 1) * blk:(sub + 1) * blk, hc]

    for sub in range(tq // blk):
        rows = slice(sub * blk, (sub + 1) * blk)
        ok = first_band if sub == 0 else band
        hcs = [slice(h * HEAD_DIM, (h + 1) * HEAD_DIM) for h in heads]
        s = [lax.dot_general(q_ref[0, rows, hc], window(kp_ref, kc_ref, sub, hc), contract_last,
                             preferred_element_type=F32) for hc in hcs]
        s = [jnp.where(ok, sh * scale, NEG_INF) for sh in s]
        m = [jnp.max(sh, axis=-1, keepdims=True) for sh in s]
        p = [jnp.exp(sh - mh) for sh, mh in zip(s, m)]
        den = [jnp.sum(ph, axis=-1, keepdims=True) for ph in p]
        o = [_dot(ph.astype(BF16), window(vp_ref, vc_ref, sub, hc)) for ph, hc in zip(p, hcs)]
        lse_rows = jnp.zeros((blk, LANES), F32)
        for h in heads:
            o_ref[0, rows, hcs[h]] = (o[h] / den[h]).astype(o_ref.dtype)
            lse_rows = jnp.where(lane == h, m[h] + jnp.log(den[h]), lse_rows)
        lse_ref[0, rows, :] = lse_rows


def _attention(qkv, dilation):
    d, length, n3 = qkv.shape
    n = n3 // 3
    tq = ATTN_Q_ROWS
    sub_per_tile = tq // ATTN_BLOCK
    prev_map = lambda col: (lambda r, i: (r, jnp.maximum(i * sub_per_tile - 1, 0), col))
    cur_map = lambda col: (lambda r, i: (r, i, col))
    return pl.pallas_call(
        _attn_kernel,
        grid=(d, length // tq),
        in_specs=[
            pl.BlockSpec((1, tq, n), cur_map(0)),
            pl.BlockSpec((1, ATTN_BLOCK, n), prev_map(1)),
            pl.BlockSpec((1, tq, n), cur_map(1)),
            pl.BlockSpec((1, ATTN_BLOCK, n), prev_map(2)),
            pl.BlockSpec((1, tq, n), cur_map(2)),
        ],
        out_specs=[
            pl.BlockSpec((1, tq, n), cur_map(0)),
            pl.BlockSpec((1, tq, LANES), cur_map(0)),
        ],
        out_shape=[
            jax.ShapeDtypeStruct((d, length, n), BF16),
            jax.ShapeDtypeStruct((d, length, LANES), F32),
        ],
        compiler_params=_params("parallel", "parallel"),
        name=f"window_attn_d{dilation}",
    )(qkv, qkv, qkv, qkv, qkv)


def _split3(a):
    hi = a.astype(BF16)
    r1 = a - hi.astype(F32)
    mid = r1.astype(BF16)
    lo = (r1 - mid.astype(F32)).astype(BF16)
    return hi, mid, lo


def _merge_kernel(x_ref, o1_ref, o4_ref, o16_ref, l1_ref, l4_ref, l16_ref,
                  p4t_ref, p16t_ref, wout_ref, out_ref):
    t, n = o1_ref.shape
    outs = [o1_ref[...].astype(F32)]
    lses = [l1_ref[...]]
    for o_ref, l_ref, pt_ref in ((o4_ref, l4_ref, p4t_ref), (o16_ref, l16_ref, p16t_ref)):
        pt = pt_ref[...]
        outs.append(_dot(pt, o_ref[...].reshape(t, n)))
        hi, mid, lo = _split3(l_ref[...].reshape(t, LANES))
        lses.append((_dot(pt, hi) + _dot(pt, mid)) + _dot(pt, lo))
    m = jnp.maximum(jnp.maximum(lses[0], lses[1]), lses[2])
    es = [jnp.exp(l - m) for l in lses]
    inv = 1.0 / (es[0] + es[1] + es[2])
    ws = [e * inv for e in es]
    cols = []
    for h in range(ATTN_HEADS):
        hc = slice(h * HEAD_DIM, (h + 1) * HEAD_DIM)
        acc = ws[0][:, h:h + 1] * outs[0][:, hc]
        acc += ws[1][:, h:h + 1] * outs[1][:, hc]
        acc += ws[2][:, h:h + 1] * outs[2][:, hc]
        cols.append(acc)
    o = jnp.concatenate(cols, axis=1).astype(BF16)
    out_ref[...] = x_ref[...] + _dot(o, wout_ref[...])


def _attn_merge(x, outs, lses, perms_t, w_out):
    s, d = x.shape
    n = w_out.shape[0]
    t = PERM_ROWS
    d4, d16 = ATTN_PATTERNS[1][1], ATTN_PATTERNS[2][1]
    grouped = lambda dil, width: pl.BlockSpec((dil, t // dil, width), lambda i: (0, i, 0))
    return pl.pallas_call(
        _merge_kernel,
        grid=(s // t,),
        in_specs=[
            pl.BlockSpec((t, d), lambda i: (i, 0)),
            pl.BlockSpec((t, n), lambda i: (i, 0)),
            grouped(d4, n),
            grouped(d16, n),
            pl.BlockSpec((t, LANES), lambda i: (i, 0)),
            grouped(d4, LANES),
            grouped(d16, LANES),
            _resident((t, t), lambda i: (0, 0)),
            _resident((t, t), lambda i: (0, 0)),
            _resident((n, d), lambda i: (0, 0)),
        ],
        out_specs=pl.BlockSpec((t, d), lambda i: (i, 0)),
        out_shape=jax.ShapeDtypeStruct((s, d), F32),
        compiler_params=_params("parallel"),
        name="attn_merge",
    )(x, outs[0], outs[1], outs[2], lses[0], lses[1], lses[2],
      perms_t[0], perms_t[1], w_out)


def _attn_mixer(x, gain, w_qkv, layer, w_out):
    s, d = x.shape
    n = ATTN_HEADS * HEAD_DIM
    perms = [_residue_permutation(PERM_ROWS, dil) for _, dil in ATTN_PATTERNS[1:]]
    p = [jnp.asarray(m, BF16) for m in perms]
    pt = [jnp.asarray(m.T, BF16) for m in perms]
    xn_groups = _norm_perm(x, gain, p)
    outs, lses = [], []
    for g, (_, dil) in enumerate(ATTN_PATTERNS):
        qkv = _qkv_proj(xn_groups[g].reshape(s, d), w_qkv, layer, g)
        o, lse = _attention(qkv.reshape(dil, s // dil, 3 * n), dil)
        if dil == 1:
            o, lse = o.reshape(s, n), lse.reshape(s, LANES)
        outs.append(o)
        lses.append(lse)
    return _attn_merge(x, outs, lses, pt, w_out)


def kernel(x, norm_mix, pool_w_in, pool_w_group, pool_scale, pool_w_out, sgu_w_in, sgu_v_norm, sgu_w_s, sgu_b_s, sgu_w_out, attn_w_qkv, attn_w_out, norm_mlp, mlp_w_up, mlp_w_down, norm_final):
    batch, seq, d = x.shape
    depth = norm_mix.shape[0]
    n_mixers = 3
    bf = lambda w: w.astype(BF16)
    row = lambda v: v.reshape(1, -1)
    outs = []
    seqs = [x.reshape(seq, d)] if batch == 1 else [x[b] for b in range(batch)]
    for h in seqs:
        for i in range(depth):
            kind, j = i % n_mixers, i // n_mixers
            gain = row(norm_mix[i])
            if kind == 0:
                h = _pool_mixer(h, gain, bf(pool_w_in[j]), bf(pool_w_group[j]),
                                row(pool_scale[j]), bf(pool_w_out[j]))
            elif kind == 1:
                h = _sgu_mixer(h, gain, bf(sgu_w_in[j]), row(sgu_v_norm[j]), bf(sgu_w_s[j]),
                               sgu_b_s[j].T, bf(sgu_w_out[j]))
            else:
                h = _attn_mixer(h, gain, attn_w_qkv, j, bf(attn_w_out[j]))
            h = _mlp(h, row(norm_mlp[i]), mlp_w_up, mlp_w_down, i,
                     row(norm_final), final_norm=(i == depth - 1))
        outs.append(h)
    return outs[0].reshape(1, seq, d) if batch == 1 else jnp.stack(outs, axis=0)
```

```python
import functools

import jax
import jax.numpy as jnp
import numpy as np
from jax import lax
from jax.experimental import pallas as pl
from jax.experimental.pallas import tpu as pltpu

F32 = jnp.float32
BF16 = jnp.bfloat16

RMS_EPS = 1e-6
POOL_WINDOWS = (2, 4, 8, 16)
SGU_CHUNK = 128
SGU_GROUPS = 8
ATTN_PATTERNS = ((128, 1), (512, 4), (2048, 16))
ATTN_HEADS = 8
HEAD_DIM = 128
ATTN_BLOCK = 128
NEG_INF = -1e30

V7X_VMEM_LIMIT_BYTES = 60 * 1024 * 1024
LANES = 128

MLP_ROWS = 1024
MLP_FF_COLS = 1024
MIX_ROWS = 512
SGU_ROWS = 256
PERM_ROWS = 256
PROJ_ROWS = 2048
ATTN_Q_ROWS = 512
POOL_HALO = 32


def _params(*semantics):
    return pltpu.CompilerParams(dimension_semantics=semantics,
                                vmem_limit_bytes=V7X_VMEM_LIMIT_BYTES)


def _resident(shape, index_map):
    return pl.BlockSpec(shape, index_map, pipeline_mode=pl.Buffered(1))


def _rmsnorm(xf, gain_row):
    ms = jnp.mean(xf * xf, axis=-1, keepdims=True)
    return (xf * lax.rsqrt(ms + RMS_EPS)) * gain_row


def _dot(a, b):
    return lax.dot_general(a, b, (((1,), (0,)), ((), ())), preferred_element_type=F32)


def _mlp_weight_cast_specs(w_up, w_down, layer, steps):
    _, d, ff = w_up.shape
    in_specs = [pl.BlockSpec((None, d // steps, ff), lambda i: (layer, i, 0)),
                pl.BlockSpec((None, ff // steps, d), lambda i: (layer, i, 0))]
    out_specs = [pl.BlockSpec((d // steps, ff), lambda i: (i, 0)),
                 pl.BlockSpec((ff // steps, d), lambda i: (i, 0))]
    out_shape = [jax.ShapeDtypeStruct((d, ff), BF16), jax.ShapeDtypeStruct((ff, d), BF16)]
    return in_specs, out_specs, out_shape


def _cast_mlp_weights(wup_ref, wdn_ref, wup_bf_ref, wdn_bf_ref):
    wup_bf_ref[...] = wup_ref[...].astype(BF16)
    wdn_bf_ref[...] = wdn_ref[...].astype(BF16)


def _mlp_kernel(x_ref, g_ref, wup_ref, wdn_ref, gf_ref, o_ref, xn_ref, *, final_norm):
    f = pl.program_id(1)

    @pl.when(f == 0)
    def _():
        x = x_ref[...]
        xn_ref[...] = _rmsnorm(x, g_ref[...]).astype(BF16)
        o_ref[...] = x

    h = jnp.maximum(_dot(xn_ref[...], wup_ref[...]), 0.0)
    o_ref[...] += _dot((h * h).astype(BF16), wdn_ref[...])

    if final_norm:
        @pl.when(f == pl.num_programs(1) - 1)
        def _():
            o_ref[...] = _rmsnorm(o_ref[...], gf_ref[...])


def _mlp(x, gain, w_up, w_down, final_gain, final_norm):
    s, d = x.shape
    ff = w_up.shape[1]
    tm, tf = MLP_ROWS, MLP_FF_COLS
    return pl.pallas_call(
        functools.partial(_mlp_kernel, final_norm=final_norm),
        grid=(s // tm, ff // tf),
        in_specs=[
            pl.BlockSpec((tm, d), lambda i, f: (i, 0)),
            pl.BlockSpec((1, d), lambda i, f: (0, 0)),
            pl.BlockSpec((d, tf), lambda i, f: (0, f)),
            pl.BlockSpec((tf, d), lambda i, f: (f, 0)),
            pl.BlockSpec((1, d), lambda i, f: (0, 0)),
        ],
        out_specs=pl.BlockSpec((tm, d), lambda i, f: (i, 0)),
        out_shape=jax.ShapeDtypeStruct((s, d), F32),
        scratch_shapes=[pltpu.VMEM((tm, d), BF16)],
        compiler_params=_params("parallel", "arbitrary"),
        name="mlp_final" if final_norm else "mlp",
    )(x, gain, w_up, w_down, final_gain)


def _pool_kernel(x_ref, g_ref, win_ref, wg_ref, sc_ref, wout_ref, wup_ref, wdn_ref,
                 o_ref, wup_bf_ref, wdn_bf_ref, hext_ref, t0_ref, t1_ref):
    b = pl.program_id(0)
    tm, d = x_ref.shape
    halo = POOL_HALO
    gc = d // len(POOL_WINDOWS)

    @pl.when(b == 0)
    def _():
        hext_ref[0:halo, :] = jnp.zeros((halo, d), F32)
        t0_ref[0:halo, :] = jnp.zeros((halo, gc), F32)
        t1_ref[0:halo, :] = jnp.zeros((halo, gc), F32)

    @pl.when(b > 0)
    def _():
        hext_ref[halo - 16:halo, :] = hext_ref[tm + halo - 16:tm + halo, :]

    x = x_ref[...]
    xn = _rmsnorm(x, g_ref[...]).astype(BF16)
    hext_ref[halo:halo + tm, :] = _dot(xn, win_ref[...])

    pos = b * tm + lax.broadcasted_iota(jnp.int32, (tm, 1), 0)
    temps = (t0_ref, t1_ref)
    mixed = []
    for g, w in enumerate(POOL_WINDOWS):
        cols = slice(g * gc, (g + 1) * gc)
        src, src_cols, shift, levels = hext_ref, cols, 1, g + 1
        for lvl in range(levels):
            last = lvl == levels - 1
            lo = halo if last else halo - 16
            n = tm if last else tm + 16
            s = src[lo:lo + n, src_cols] + src[lo - shift:lo - shift + n, src_cols]
            if not last:
                dst = temps[lvl % 2]
                dst[lo:lo + n, :] = s
                src, src_cols = dst, slice(None)
            shift *= 2
        inv_count = 1.0 / jnp.minimum(pos + 1, w).astype(F32)
        pooled = s * inv_count - hext_ref[halo:halo + tm, cols]
        mixed.append(_dot(pooled.astype(BF16), wg_ref[g]))
    mixed = (jnp.concatenate(mixed, axis=1) * sc_ref[...]).astype(BF16)
    o_ref[...] = x + _dot(mixed, wout_ref[...])
    _cast_mlp_weights(wup_ref, wdn_ref, wup_bf_ref, wdn_bf_ref)


def _pool_mixer(x, gain, w_in, w_group, scale, w_out, mlp_w_up, mlp_w_down, layer):
    s, d = x.shape
    tm = MIX_ROWS
    ng, gc, _ = w_group.shape
    steps = s // tm
    cast_in, cast_out, cast_shape = _mlp_weight_cast_specs(mlp_w_up, mlp_w_down, layer, steps)
    return pl.pallas_call(
        _pool_kernel,
        grid=(steps,),
        in_specs=[
            pl.BlockSpec((tm, d), lambda i: (i, 0)),
            _resident((1, d), lambda i: (0, 0)),
            _resident((d, d), lambda i: (0, 0)),
            _resident((ng, gc, gc), lambda i: (0, 0, 0)),
            _resident((1, d), lambda i: (0, 0)),
            _resident((d, d), lambda i: (0, 0)),
        ] + cast_in,
        out_specs=[pl.BlockSpec((tm, d), lambda i: (i, 0))] + cast_out,
        out_shape=[jax.ShapeDtypeStruct((s, d), F32)] + cast_shape,
        scratch_shapes=[pltpu.VMEM((tm + POOL_HALO, d), F32),
                        pltpu.VMEM((tm + POOL_HALO, gc), F32),
                        pltpu.VMEM((tm + POOL_HALO, gc), F32)],
        compiler_params=_params("arbitrary"),
        name="pool_mixer",
    )(x, gain, w_in, w_group, scale, w_out, mlp_w_up, mlp_w_down)


def _gelu(x):
    return 0.5 * x * (1.0 + lax.erf(x * np.float32(np.sqrt(0.5))))


def _sgu_kernel(x_ref, g_ref, win_ref, vn_ref, ws_ref, bs_ref, wout_ref, wup_ref, wdn_ref,
                o_ref, wup_bf_ref, wdn_bf_ref, u_ref, v_ref, gated_ref):
    tm, d = x_ref.shape
    e = win_ref.shape[1] // 2
    gcols = e // SGU_GROUPS
    x = x_ref[...]
    xn = _rmsnorm(x, g_ref[...]).astype(BF16)
    u_ref[...] = _gelu(_dot(xn, win_ref[:, :e]))
    v = _gelu(_dot(xn, win_ref[:, e:]))
    v_ref[...] = _rmsnorm(v, vn_ref[...]).astype(BF16)

    t = lax.broadcasted_iota(jnp.int32, (SGU_CHUNK, SGU_CHUNK), 0)
    sidx = lax.broadcasted_iota(jnp.int32, (SGU_CHUNK, SGU_CHUNK), 1)
    causal = sidx <= t
    for g in range(SGU_GROUPS):
        ws = jnp.where(causal, ws_ref[g], jnp.zeros_like(ws_ref[g]))
        bias = bs_ref[:, g:g + 1]
        cols = slice(g * gcols, (g + 1) * gcols)
        for c in range(tm // SGU_CHUNK):
            rows = slice(c * SGU_CHUNK, (c + 1) * SGU_CHUNK)
            sp = _dot(ws, v_ref[rows, cols]) + bias
            gated_ref[rows, cols] = (u_ref[rows, cols] * sp).astype(BF16)
    o_ref[...] = x + _dot(gated_ref[...], wout_ref[...])
    _cast_mlp_weights(wup_ref, wdn_ref, wup_bf_ref, wdn_bf_ref)


def _sgu_mixer(x, gain, w_in, v_norm, w_s, b_s_t, w_out, mlp_w_up, mlp_w_down, layer):
    s, d = x.shape
    tm = SGU_ROWS
    e = w_out.shape[0]
    steps = s // tm
    cast_in, cast_out, cast_shape = _mlp_weight_cast_specs(mlp_w_up, mlp_w_down, layer, steps)
    return pl.pallas_call(
        _sgu_kernel,
        grid=(steps,),
        in_specs=[
            pl.BlockSpec((tm, d), lambda i: (i, 0)),
            _resident((1, d), lambda i: (0, 0)),
            _resident((d, 2 * e), lambda i: (0, 0)),
            _resident((1, e), lambda i: (0, 0)),
            _resident(w_s.shape, lambda i: (0, 0, 0)),
            _resident(b_s_t.shape, lambda i: (0, 0)),
            _resident((e, d), lambda i: (0, 0)),
        ] + cast_in,
        out_specs=[pl.BlockSpec((tm, d), lambda i: (i, 0))] + cast_out,
        out_shape=[jax.ShapeDtypeStruct((s, d), F32)] + cast_shape,
        scratch_shapes=[pltpu.VMEM((tm, e), F32),
                        pltpu.VMEM((tm, e), BF16),
                        pltpu.VMEM((tm, e), BF16)],
        compiler_params=_params("parallel"),
        name="sgu_mixer",
    )(x, gain, w_in, v_norm, w_s, b_s_t, w_out, mlp_w_up, mlp_w_down)


def _residue_permutation(rows, dilation):
    n = np.arange(rows)
    per = rows // dilation
    old = dilation * (n % per) + n // per
    p = np.zeros((rows, rows), np.float32)
    p[n, old] = 1.0
    return p


def _norm_perm_kernel(x_ref, g_ref, p4_ref, p16_ref, o1_ref, o4_ref, o16_ref):
    xn = _rmsnorm(x_ref[...], g_ref[...]).astype(BF16)
    o1_ref[...] = xn
    for p_ref, o_ref in ((p4_ref, o4_ref), (p16_ref, o16_ref)):
        y = _dot(p_ref[...], xn).astype(BF16)
        o_ref[...] = y.reshape(o_ref.shape)


def _norm_perm(x, gain, perms):
    s, d = x.shape
    t = PERM_ROWS
    d4, d16 = ATTN_PATTERNS[1][1], ATTN_PATTERNS[2][1]
    return pl.pallas_call(
        _norm_perm_kernel,
        grid=(s // t,),
        in_specs=[
            pl.BlockSpec((t, d), lambda i: (i, 0)),
            _resident((1, d), lambda i: (0, 0)),
            _resident((t, t), lambda i: (0, 0)),
            _resident((t, t), lambda i: (0, 0)),
        ],
        out_specs=[
            pl.BlockSpec((t, d), lambda i: (i, 0)),
            pl.BlockSpec((d4, t // d4, d), lambda i: (0, i, 0)),
            pl.BlockSpec((d16, t // d16, d), lambda i: (0, i, 0)),
        ],
        out_shape=[
            jax.ShapeDtypeStruct((s, d), BF16),
            jax.ShapeDtypeStruct((d4, s // d4, d), BF16),
            jax.ShapeDtypeStruct((d16, s // d16, d), BF16),
        ],
        compiler_params=_params("parallel"),
        name="attn_norm_perm",
    )(x, gain, perms[0], perms[1])


def _proj_kernel(a_ref, w_ref, o_ref):
    o_ref[...] = _dot(a_ref[...], w_ref[...]).astype(o_ref.dtype)


def _qkv_proj(a, w_qkv, layer, group):
    s, d = a.shape
    n = ATTN_HEADS * HEAD_DIM
    n_groups = len(ATTN_PATTERNS)
    tm = PROJ_ROWS
    return pl.pallas_call(
        _proj_kernel,
        grid=(3, s // tm),
        in_specs=[
            pl.BlockSpec((tm, d), lambda j, i: (i, 0)),
            pl.BlockSpec((None, d, n), lambda j, i: (layer, 0, j * n_groups + group)),
        ],
        out_specs=pl.BlockSpec((tm, n), lambda j, i: (i, j)),
        out_shape=jax.ShapeDtypeStruct((s, 3 * n), BF16),
        compiler_params=_params("parallel", "parallel"),
        name=f"qkv_proj_g{group}",
    )(a, w_qkv)


def _attn_kernel(q_ref, kp_ref, kc_ref, vp_ref, vc_ref, o_ref, lse_ref):
    ib = pl.program_id(1)
    tq = q_ref.shape[1]
    blk = ATTN_BLOCK
    scale = HEAD_DIM ** -0.5
    heads = range(ATTN_HEADS)
    row = lax.broadcasted_iota(jnp.int32, (blk, 2 * blk), 0)
    col = lax.broadcasted_iota(jnp.int32, (blk, 2 * blk), 1)
    band = jnp.logical_and(col >= row, col <= row + blk)
    first_band = jnp.logical_and(band, jnp.logical_or(col >= blk, ib > 0))
    lane = lax.broadcasted_iota(jnp.int32, (blk, LANES), 1)
    contract_last = (((1,), (1,)), ((), ()))

    def window(prev_ref, cur_ref, sub, hc):
        if sub == 0:
            return jnp.concatenate([prev_ref[0, :, hc], cur_ref[0, 0:blk, hc]], axis=0)
        return cur_ref[0, (sub - 1) * blk:(sub + 1) * blk, hc]

    for sub in range(tq // blk):
        rows = slice(sub * blk, (sub + 1) * blk)
        ok = first_band if sub == 0 else band
        hcs = [slice(h * HEAD_DIM, (h + 1) * HEAD_DIM) for h in heads]
        s = [lax.dot_general(q_ref[0, rows, hc], window(kp_ref, kc_ref, sub, hc), contract_last,
                             preferred_element_type=F32) for hc in hcs]
        s = [jnp.where(ok, sh * scale, NEG_INF) for sh in s]
        m = [jnp.max(sh, axis=-1, keepdims=True) for sh in s]
        p = [jnp.exp(sh - mh) for sh, mh in zip(s, m)]
        den = [jnp.sum(ph, axis=-1, keepdims=True) for ph in p]
        o = [_dot(ph.astype(BF16), window(vp_ref, vc_ref, sub, hc)) for ph, hc in zip(p, hcs)]
        lse_rows = jnp.zeros((blk, LANES), F32)
        for h in heads:
            o_ref[0, rows, hcs[h]] = (o[h] / den[h]).astype(o_ref.dtype)
            lse_rows = jnp.where(lane == h, m[h] + jnp.log(den[h]), lse_rows)
        lse_ref[0, rows, :] = lse_rows


def _attention(qkv, dilation):
    d, length, n3 = qkv.shape
    n = n3 // 3
    tq = ATTN_Q_ROWS
    sub_per_tile = tq // ATTN_BLOCK
    prev_map = lambda col: (lambda r, i: (r, jnp.maximum(i * sub_per_tile - 1, 0), col))
    cur_map = lambda col: (lambda r, i: (r, i, col))
    return pl.pallas_call(
        _attn_kernel,
        grid=(d, length // tq),
        in_specs=[
            pl.BlockSpec((1, tq, n), cur_map(0)),
            pl.BlockSpec((1, ATTN_BLOCK, n), prev_map(1)),
            pl.BlockSpec((1, tq, n), cur_map(1)),
            pl.BlockSpec((1, ATTN_BLOCK, n), prev_map(2)),
            pl.BlockSpec((1, tq, n), cur_map(2)),
        ],
        out_specs=[
            pl.BlockSpec((1, tq, n), cur_map(0)),
            pl.BlockSpec((1, tq, LANES), cur_map(0)),
        ],
        out_shape=[
            jax.ShapeDtypeStruct((d, length, n), BF16),
            jax.ShapeDtypeStruct((d, length, LANES), F32),
        ],
        compiler_params=_params("parallel", "parallel"),
        name=f"window_attn_d{dilation}",
    )(qkv, qkv, qkv, qkv, qkv)


def _split3(a):
    hi = a.astype(BF16)
    r1 = a - hi.astype(F32)
    mid = r1.astype(BF16)
    lo = (r1 - mid.astype(F32)).astype(BF16)
    return hi, mid, lo


def _merge_kernel(x_ref, o1_ref, o4_ref, o16_ref, l1_ref, l4_ref, l16_ref,
                  p4t_ref, p16t_ref, wout_ref, wup_ref, wdn_ref,
                  out_ref, wup_bf_ref, wdn_bf_ref):
    t, n = o1_ref.shape
    outs = [o1_ref[...].astype(F32)]
    lses = [l1_ref[...]]
    for o_ref, l_ref, pt_ref in ((o4_ref, l4_ref, p4t_ref), (o16_ref, l16_ref, p16t_ref)):
        pt = pt_ref[...]
        outs.append(_dot(pt, o_ref[...].reshape(t, n)))
        hi, mid, lo = _split3(l_ref[...].reshape(t, LANES))
        lses.append((_dot(pt, hi) + _dot(pt, mid)) + _dot(pt, lo))
    m = jnp.maximum(jnp.maximum(lses[0], lses[1]), lses[2])
    es = [jnp.exp(l - m) for l in lses]
    inv = 1.0 / (es[0] + es[1] + es[2])
    ws = [e * inv for e in es]
    cols = []
    for h in range(ATTN_HEADS):
        hc = slice(h * HEAD_DIM, (h + 1) * HEAD_DIM)
        acc = ws[0][:, h:h + 1] * outs[0][:, hc]
        acc += ws[1][:, h:h + 1] * outs[1][:, hc]
        acc += ws[2][:, h:h + 1] * outs[2][:, hc]
        cols.append(acc)
    o = jnp.concatenate(cols, axis=1).astype(BF16)
    out_ref[...] = x_ref[...] + _dot(o, wout_ref[...])
    _cast_mlp_weights(wup_ref, wdn_ref, wup_bf_ref, wdn_bf_ref)


def _attn_merge(x, outs, lses, perms_t, w_out, mlp_w_up, mlp_w_down, layer):
    s, d = x.shape
    n = w_out.shape[0]
    t = PERM_ROWS
    d4, d16 = ATTN_PATTERNS[1][1], ATTN_PATTERNS[2][1]
    grouped = lambda dil, width: pl.BlockSpec((dil, t // dil, width), lambda i: (0, i, 0))
    steps = s // t
    cast_in, cast_out, cast_shape = _mlp_weight_cast_specs(mlp_w_up, mlp_w_down, layer, steps)
    return pl.pallas_call(
        _merge_kernel,
        grid=(steps,),
        in_specs=[
            pl.BlockSpec((t, d), lambda i: (i, 0)),
            pl.BlockSpec((t, n), lambda i: (i, 0)),
            grouped(d4, n),
            grouped(d16, n),
            pl.BlockSpec((t, LANES), lambda i: (i, 0)),
            grouped(d4, LANES),
            grouped(d16, LANES),
            _resident((t, t), lambda i: (0, 0)),
            _resident((t, t), lambda i: (0, 0)),
            _resident((n, d), lambda i: (0, 0)),
        ] + cast_in,
        out_specs=[pl.BlockSpec((t, d), lambda i: (i, 0))] + cast_out,
        out_shape=[jax.ShapeDtypeStruct((s, d), F32)] + cast_shape,
        compiler_params=_params("parallel"),
        name="attn_merge",
    )(x, outs[0], outs[1], outs[2], lses[0], lses[1], lses[2],
      perms_t[0], perms_t[1], w_out, mlp_w_up, mlp_w_down)


def _attn_mixer(x, gain, w_qkv, attn_layer, w_out, mlp_w_up, mlp_w_down, layer):
    s, d = x.shape
    n = ATTN_HEADS * HEAD_DIM
    perms = [_residue_permutation(PERM_ROWS, dil) for _, dil in ATTN_PATTERNS[1:]]
    p = [jnp.asarray(m, BF16) for m in perms]
    pt = [jnp.asarray(m.T, BF16) for m in perms]
    xn_groups = _norm_perm(x, gain, p)
    outs, lses = [], []
    for g, (_, dil) in enumerate(ATTN_PATTERNS):
        qkv = _qkv_proj(xn_groups[g].reshape(s, d), w_qkv, attn_layer, g)
        o, lse = _attention(qkv.reshape(dil, s // dil, 3 * n), dil)
        if dil == 1:
            o, lse = o.reshape(s, n), lse.reshape(s, LANES)
        outs.append(o)
        lses.append(lse)
    return _attn_merge(x, outs, lses, pt, w_out, mlp_w_up, mlp_w_down, layer)


def kernel(x, norm_mix, pool_w_in, pool_w_group, pool_scale, pool_w_out, sgu_w_in, sgu_v_norm, sgu_w_s, sgu_b_s, sgu_w_out, attn_w_qkv, attn_w_out, norm_mlp, mlp_w_up, mlp_w_down, norm_final):
    batch, seq, d = x.shape
    depth = norm_mix.shape[0]
    n_mixers = 3
    bf = lambda w: w.astype(BF16)
    row = lambda v: v.reshape(1, -1)
    outs = []
    seqs = [x.reshape(seq, d)] if batch == 1 else [x[b] for b in range(batch)]
    for h in seqs:
        for i in range(depth):
            kind, j = i % n_mixers, i // n_mixers
            gain = row(norm_mix[i])
            mlp_w = (mlp_w_up, mlp_w_down, i)
            if kind == 0:
                h, w_up, w_down = _pool_mixer(h, gain, bf(pool_w_in[j]), bf(pool_w_group[j]),
                                              row(pool_scale[j]), bf(pool_w_out[j]), *mlp_w)
            elif kind == 1:
                h, w_up, w_down = _sgu_mixer(h, gain, bf(sgu_w_in[j]), row(sgu_v_norm[j]),
                                             bf(sgu_w_s[j]), sgu_b_s[j].T, bf(sgu_w_out[j]),
                                             *mlp_w)
            else:
                h, w_up, w_down = _attn_mixer(h, gain, attn_w_qkv, j, bf(attn_w_out[j]), *mlp_w)
            h = _mlp(h, row(norm_mlp[i]), w_up, w_down,
                     row(norm_final), final_norm=(i == depth - 1))
        outs.append(h)
    return outs[0].reshape(1, seq, d) if batch == 1 else jnp.stack(outs, axis=0)
```

```python
import functools

import jax
import jax.numpy as jnp
import numpy as np
from jax import lax
from jax.experimental import pallas as pl
from jax.experimental.pallas import tpu as pltpu

F32 = jnp.float32
BF16 = jnp.bfloat16

RMS_EPS = 1e-6
POOL_WINDOWS = (2, 4, 8, 16)
SGU_CHUNK = 128
SGU_GROUPS = 8
ATTN_PATTERNS = ((128, 1), (512, 4), (2048, 16))
ATTN_HEADS = 8
HEAD_DIM = 128
ATTN_BLOCK = 128
NEG_INF = -1e30

V7X_VMEM_LIMIT_BYTES = 62 * 1024 * 1024
LANES = 128

MLP_ROWS = 1024
MLP_FF_COLS = 1024
MIX_ROWS = 512
SGU_ROWS = 512
PERM_ROWS = 256
MERGE_ROWS = 512
PROJ_ROWS = 2048
ATTN_Q_ROWS = 512
POOL_HALO = 32


def _params(*semantics):
    return pltpu.CompilerParams(dimension_semantics=semantics,
                                vmem_limit_bytes=V7X_VMEM_LIMIT_BYTES)


def _resident(shape, index_map):
    return pl.BlockSpec(shape, index_map, pipeline_mode=pl.Buffered(1))


def _rmsnorm(xf, gain_row):
    ms = jnp.mean(xf * xf, axis=-1, keepdims=True)
    return (xf * lax.rsqrt(ms + RMS_EPS)) * gain_row


def _dot(a, b):
    return lax.dot_general(a, b, (((1,), (0,)), ((), ())), preferred_element_type=F32)


def _mlp_weight_cast_specs(w_up, w_down, layer, steps, step_of):
    _, d, ff = w_up.shape
    in_specs = [pl.BlockSpec((None, d // steps, ff), lambda *g: (layer, step_of(*g), 0)),
                pl.BlockSpec((None, ff // steps, d), lambda *g: (layer, step_of(*g), 0))]
    out_specs = [pl.BlockSpec((d // steps, ff), lambda *g: (step_of(*g), 0)),
                 pl.BlockSpec((ff // steps, d), lambda *g: (step_of(*g), 0))]
    out_shape = [jax.ShapeDtypeStruct((d, ff), BF16), jax.ShapeDtypeStruct((ff, d), BF16)]
    return in_specs, out_specs, out_shape


def _cast_mlp_weights(wup_ref, wdn_ref, wup_bf_ref, wdn_bf_ref):
    wup_bf_ref[...] = wup_ref[...].astype(BF16)
    wdn_bf_ref[...] = wdn_ref[...].astype(BF16)


def _mlp_kernel(x_ref, g_ref, wup_ref, wdn_ref, gf_ref, *refs, last):
    if last:
        o_ref, xn_ref = refs
    else:
        nup_ref, ndn_ref, o_ref, nup_bf_ref, ndn_bf_ref, xn_ref = refs
    f = pl.program_id(1)

    @pl.when(f == 0)
    def _():
        x = x_ref[...]
        xn_ref[...] = _rmsnorm(x, g_ref[...]).astype(BF16)
        o_ref[...] = x

    h = jnp.maximum(_dot(xn_ref[...], wup_ref[...]), 0.0)
    o_ref[...] += _dot((h * h).astype(BF16), wdn_ref[...])

    if last:
        @pl.when(f == pl.num_programs(1) - 1)
        def _():
            o_ref[...] = _rmsnorm(o_ref[...], gf_ref[...])
    else:
        _cast_mlp_weights(nup_ref, ndn_ref, nup_bf_ref, ndn_bf_ref)


def _mlp(x, gain, w_up, w_down, final_gain, next_weights):
    s, d = x.shape
    ff = w_up.shape[1]
    tm, tf = MLP_ROWS, MLP_FF_COLS
    last = next_weights is None
    n_chunks = ff // tf
    in_specs = [
        pl.BlockSpec((tm, d), lambda i, f: (i, 0)),
        pl.BlockSpec((1, d), lambda i, f: (0, 0)),
        pl.BlockSpec((d, tf), lambda i, f: (0, f)),
        pl.BlockSpec((tf, d), lambda i, f: (f, 0)),
        pl.BlockSpec((1, d), lambda i, f: (0, 0)),
    ]
    out_specs = [pl.BlockSpec((tm, d), lambda i, f: (i, 0))]
    out_shape = [jax.ShapeDtypeStruct((s, d), F32)]
    operands = [x, gain, w_up, w_down, final_gain]
    if not last:
        cast_in, cast_out, cast_shape = _mlp_weight_cast_specs(
            *next_weights, (s // tm) * n_chunks, lambda i, f: i * n_chunks + f)
        in_specs += cast_in
        out_specs += cast_out
        out_shape += cast_shape
        operands += list(next_weights[:2])
    return pl.pallas_call(
        functools.partial(_mlp_kernel, last=last),
        grid=(s // tm, n_chunks),
        in_specs=in_specs,
        out_specs=out_specs,
        out_shape=out_shape,
        scratch_shapes=[pltpu.VMEM((tm, d), BF16)],
        compiler_params=_params("parallel", "arbitrary"),
        name="mlp_final" if last else "mlp",
    )(*operands)


def _pool_kernel(x_ref, g_ref, win_ref, wg_ref, sc_ref, wout_ref, *refs, cast):
    if cast:
        wup_ref, wdn_ref, o_ref, wup_bf_ref, wdn_bf_ref, hext_ref, t0_ref, t1_ref = refs
    else:
        o_ref, hext_ref, t0_ref, t1_ref = refs
    b = pl.program_id(0)
    tm, d = x_ref.shape
    halo = POOL_HALO
    gc = d // len(POOL_WINDOWS)

    @pl.when(b == 0)
    def _():
        hext_ref[0:halo, :] = jnp.zeros((halo, d), F32)
        t0_ref[0:halo, :] = jnp.zeros((halo, gc), F32)
        t1_ref[0:halo, :] = jnp.zeros((halo, gc), F32)

    @pl.when(b > 0)
    def _():
        hext_ref[halo - 16:halo, :] = hext_ref[tm + halo - 16:tm + halo, :]

    x = x_ref[...]
    xn = _rmsnorm(x, g_ref[...]).astype(BF16)
    hext_ref[halo:halo + tm, :] = _dot(xn, win_ref[...])

    pos = b * tm + lax.broadcasted_iota(jnp.int32, (tm, 1), 0)
    temps = (t0_ref, t1_ref)
    mixed = []
    for g, w in enumerate(POOL_WINDOWS):
        cols = slice(g * gc, (g + 1) * gc)
        src, src_cols, shift, levels = hext_ref, cols, 1, g + 1
        for lvl in range(levels):
            last = lvl == levels - 1
            lo = halo if last else halo - 16
            n = tm if last else tm + 16
            s = src[lo:lo + n, src_cols] + src[lo - shift:lo - shift + n, src_cols]
            if not last:
                dst = temps[lvl % 2]
                dst[lo:lo + n, :] = s
                src, src_cols = dst, slice(None)
            shift *= 2
        inv_count = 1.0 / jnp.minimum(pos + 1, w).astype(F32)
        pooled = s * inv_count - hext_ref[halo:halo + tm, cols]
        mixed.append(_dot(pooled.astype(BF16), wg_ref[g]))
    mixed = (jnp.concatenate(mixed, axis=1) * sc_ref[...]).astype(BF16)
    o_ref[...] = x + _dot(mixed, wout_ref[...])
    if cast:
        _cast_mlp_weights(wup_ref, wdn_ref, wup_bf_ref, wdn_bf_ref)


def _pool_mixer(x, gain, w_in, w_group, scale, w_out, cast_weights):
    s, d = x.shape
    tm = MIX_ROWS
    ng, gc, _ = w_group.shape
    steps = s // tm
    in_specs = [
        pl.BlockSpec((tm, d), lambda i: (i, 0)),
        _resident((1, d), lambda i: (0, 0)),
        _resident((d, d), lambda i: (0, 0)),
        _resident((ng, gc, gc), lambda i: (0, 0, 0)),
        _resident((1, d), lambda i: (0, 0)),
        _resident((d, d), lambda i: (0, 0)),
    ]
    out_specs = [pl.BlockSpec((tm, d), lambda i: (i, 0))]
    out_shape = [jax.ShapeDtypeStruct((s, d), F32)]
    operands = [x, gain, w_in, w_group, scale, w_out]
    cast = cast_weights is not None
    if cast:
        cast_in, cast_out, cast_shape = _mlp_weight_cast_specs(*cast_weights, steps, lambda i: i)
        in_specs += cast_in
        out_specs += cast_out
        out_shape += cast_shape
        operands += list(cast_weights[:2])
    return pl.pallas_call(
        functools.partial(_pool_kernel, cast=cast),
        grid=(steps,),
        in_specs=in_specs,
        out_specs=out_specs,
        out_shape=out_shape,
        scratch_shapes=[pltpu.VMEM((tm + POOL_HALO, d), F32),
                        pltpu.VMEM((tm + POOL_HALO, gc), F32),
                        pltpu.VMEM((tm + POOL_HALO, gc), F32)],
        compiler_params=_params("arbitrary"),
        name="pool_mixer",
    )(*operands)


def _gelu(x):
    return 0.5 * x * (1.0 + lax.erf(x * np.float32(np.sqrt(0.5))))


def _sgu_kernel(x_ref, g_ref, win_ref, vn_ref, ws_ref, bs_ref, wout_ref, o_ref,
                u_ref, v_ref, gated_ref):
    tm, d = x_ref.shape
    e = win_ref.shape[1] // 2
    gcols = e // SGU_GROUPS
    x = x_ref[...]
    xn = _rmsnorm(x, g_ref[...]).astype(BF16)
    u_ref[...] = _gelu(_dot(xn, win_ref[:, :e]))
    v = _gelu(_dot(xn, win_ref[:, e:]))
    v_ref[...] = _rmsnorm(v, vn_ref[...]).astype(BF16)

    t = lax.broadcasted_iota(jnp.int32, (SGU_CHUNK, SGU_CHUNK), 0)
    sidx = lax.broadcasted_iota(jnp.int32, (SGU_CHUNK, SGU_CHUNK), 1)
    causal = sidx <= t
    for g in range(SGU_GROUPS):
        ws = jnp.where(causal, ws_ref[g], jnp.zeros_like(ws_ref[g]))
        bias = bs_ref[:, g:g + 1]
        cols = slice(g * gcols, (g + 1) * gcols)
        for c in range(tm // SGU_CHUNK):
            rows = slice(c * SGU_CHUNK, (c + 1) * SGU_CHUNK)
            sp = _dot(ws, v_ref[rows, cols]) + bias
            gated_ref[rows, cols] = (u_ref[rows, cols] * sp).astype(BF16)
    o_ref[...] = x + _dot(gated_ref[...], wout_ref[...])


def _sgu_mixer(x, gain, w_in, v_norm, w_s, b_s_t, w_out):
    s, d = x.shape
    tm = SGU_ROWS
    e = w_out.shape[0]
    return pl.pallas_call(
        _sgu_kernel,
        grid=(s // tm,),
        in_specs=[
            pl.BlockSpec((tm, d), lambda i: (i, 0)),
            _resident((1, d), lambda i: (0, 0)),
            _resident((d, 2 * e), lambda i: (0, 0)),
            _resident((1, e), lambda i: (0, 0)),
            _resident(w_s.shape, lambda i: (0, 0, 0)),
            _resident(b_s_t.shape, lambda i: (0, 0)),
            _resident((e, d), lambda i: (0, 0)),
        ],
        out_specs=pl.BlockSpec((tm, d), lambda i: (i, 0)),
        out_shape=jax.ShapeDtypeStruct((s, d), F32),
        scratch_shapes=[pltpu.VMEM((tm, e), F32),
                        pltpu.VMEM((tm, e), BF16),
                        pltpu.VMEM((tm, e), BF16)],
        compiler_params=_params("parallel"),
        name="sgu_mixer",
    )(x, gain, w_in, v_norm, w_s, b_s_t, w_out)


def _residue_permutation(rows, dilation):
    n = np.arange(rows)
    per = rows // dilation
    old = dilation * (n % per) + n // per
    p = np.zeros((rows, rows), np.float32)
    p[n, old] = 1.0
    return p


def _norm_perm_kernel(x_ref, g_ref, p4_ref, p16_ref, o1_ref, o4_ref, o16_ref):
    xn = _rmsnorm(x_ref[...], g_ref[...]).astype(BF16)
    o1_ref[...] = xn
    for p_ref, o_ref in ((p4_ref, o4_ref), (p16_ref, o16_ref)):
        y = _dot(p_ref[...], xn).astype(BF16)
        o_ref[...] = y.reshape(o_ref.shape)


def _norm_perm(x, gain, perms):
    s, d = x.shape
    t = PERM_ROWS
    d4, d16 = ATTN_PATTERNS[1][1], ATTN_PATTERNS[2][1]
    return pl.pallas_call(
        _norm_perm_kernel,
        grid=(s // t,),
        in_specs=[
            pl.BlockSpec((t, d), lambda i: (i, 0)),
            _resident((1, d), lambda i: (0, 0)),
            _resident((t, t), lambda i: (0, 0)),
            _resident((t, t), lambda i: (0, 0)),
        ],
        out_specs=[
            pl.BlockSpec((t, d), lambda i: (i, 0)),
            pl.BlockSpec((d4, t // d4, d), lambda i: (0, i, 0)),
            pl.BlockSpec((d16, t // d16, d), lambda i: (0, i, 0)),
        ],
        out_shape=[
            jax.ShapeDtypeStruct((s, d), BF16),
            jax.ShapeDtypeStruct((d4, s // d4, d), BF16),
            jax.ShapeDtypeStruct((d16, s // d16, d), BF16),
        ],
        compiler_params=_params("parallel"),
        name="attn_norm_perm",
    )(x, gain, perms[0], perms[1])


def _proj_kernel(a_ref, w_ref, o_ref):
    o_ref[...] = _dot(a_ref[...], w_ref[...]).astype(o_ref.dtype)


def _qkv_proj(a, w_qkv, layer, group):
    s, d = a.shape
    n = ATTN_HEADS * HEAD_DIM
    n_groups = len(ATTN_PATTERNS)
    tm = PROJ_ROWS
    return pl.pallas_call(
        _proj_kernel,
        grid=(3, s // tm),
        in_specs=[
            pl.BlockSpec((tm, d), lambda j, i: (i, 0)),
            pl.BlockSpec((None, d, n), lambda j, i: (layer, 0, j * n_groups + group)),
        ],
        out_specs=pl.BlockSpec((tm, n), lambda j, i: (i, j)),
        out_shape=jax.ShapeDtypeStruct((s, 3 * n), BF16),
        compiler_params=_params("parallel", "parallel"),
        name=f"qkv_proj_g{group}",
    )(a, w_qkv)


def _attn_kernel(q_ref, kp_ref, kc_ref, vp_ref, vc_ref, o_ref, lse_ref):
    ib = pl.program_id(1)
    tq = q_ref.shape[1]
    blk = ATTN_BLOCK
    scale = HEAD_DIM ** -0.5
    heads = range(ATTN_HEADS)
    row = lax.broadcasted_iota(jnp.int32, (blk, 2 * blk), 0)
    col = lax.broadcasted_iota(jnp.int32, (blk, 2 * blk), 1)
    band = jnp.logical_and(col >= row, col <= row + blk)
    first_band = jnp.logical_and(band, jnp.logical_or(col >= blk, ib > 0))
    lane = lax.broadcasted_iota(jnp.int32, (blk, LANES), 1)
    contract_last = (((1,), (1,)), ((), ()))

    def window(prev_ref, cur_ref, sub, hc):
        if sub == 0:
            return jnp.concatenate([prev_ref[0, :, hc], cur_ref[0, 0:blk, hc]], axis=0)
        return cur_ref[0, (sub - 1) * blk:(sub + 1) * blk, hc]

    for sub in range(tq // blk):
        rows = slice(sub * blk, (sub + 1) * blk)
        ok = first_band if sub == 0 else band
        hcs = [slice(h * HEAD_DIM, (h + 1) * HEAD_DIM) for h in heads]
        s = [lax.dot_general(q_ref[0, rows, hc], window(kp_ref, kc_ref, sub, hc), contract_last,
                             preferred_element_type=F32) for hc in hcs]
        s = [jnp.where(ok, sh * scale, NEG_INF) for sh in s]
        m = [jnp.max(sh, axis=-1, keepdims=True) for sh in s]
        p = [jnp.exp(sh - mh) for sh, mh in zip(s, m)]
        den = [jnp.sum(ph, axis=-1, keepdims=True) for ph in p]
        o = [_dot(ph.astype(BF16), window(vp_ref, vc_ref, sub, hc)) for ph, hc in zip(p, hcs)]
        lse_rows = jnp.zeros((blk, LANES), F32)
        for h in heads:
            o_ref[0, rows, hcs[h]] = (o[h] / den[h]).astype(o_ref.dtype)
            lse_rows = jnp.where(lane == h, m[h] + jnp.log(den[h]), lse_rows)
        lse_ref[0, rows, :] = lse_rows


def _attention(qkv, dilation):
    d, length, n3 = qkv.shape
    n = n3 // 3
    tq = ATTN_Q_ROWS
    sub_per_tile = tq // ATTN_BLOCK
    prev_map = lambda col: (lambda r, i: (r, jnp.maximum(i * sub_per_tile - 1, 0), col))
    cur_map = lambda col: (lambda r, i: (r, i, col))
    return pl.pallas_call(
        _attn_kernel,
        grid=(d, length // tq),
        in_specs=[
            pl.BlockSpec((1, tq, n), cur_map(0)),
            pl.BlockSpec((1, ATTN_BLOCK, n), prev_map(1)),
            pl.BlockSpec((1, tq, n), cur_map(1)),
            pl.BlockSpec((1, ATTN_BLOCK, n), prev_map(2)),
            pl.BlockSpec((1, tq, n), cur_map(2)),
        ],
        out_specs=[
            pl.BlockSpec((1, tq, n), cur_map(0)),
            pl.BlockSpec((1, tq, LANES), cur_map(0)),
        ],
        out_shape=[
            jax.ShapeDtypeStruct((d, length, n), BF16),
            jax.ShapeDtypeStruct((d, length, LANES), F32),
        ],
        compiler_params=_params("parallel", "parallel"),
        name=f"window_attn_d{dilation}",
    )(qkv, qkv, qkv, qkv, qkv)


def _split3(a):
    hi = a.astype(BF16)
    r1 = a - hi.astype(F32)
    mid = r1.astype(BF16)
    lo = (r1 - mid.astype(F32)).astype(BF16)
    return hi, mid, lo


def _merge_kernel(x_ref, o1_ref, o4_ref, o16_ref, l1_ref, l4_ref, l16_ref,
                  p4t_ref, p16t_ref, wout_ref, out_ref):
    t, n = o1_ref.shape
    pr = PERM_ROWS
    subs = range(t // pr)
    heads = range(ATTN_HEADS)

    def regrouped_rows(ref, dil, sub):
        per = pr // dil
        return ref[:, sub * per:(sub + 1) * per, :].reshape(pr, ref.shape[2])

    outs, lses = [], []
    for sub in subs:
        rows = slice(sub * pr, (sub + 1) * pr)
        o_sub, l_sub = [o1_ref[rows, :].astype(F32)], [l1_ref[rows, :]]
        for o_ref, l_ref, pt_ref, dil in ((o4_ref, l4_ref, p4t_ref, ATTN_PATTERNS[1][1]),
                                          (o16_ref, l16_ref, p16t_ref, ATTN_PATTERNS[2][1])):
            pt = pt_ref[...]
            o_sub.append(_dot(pt, regrouped_rows(o_ref, dil, sub)))
            hi, mid, lo = _split3(regrouped_rows(l_ref, dil, sub))
            l_sub.append((_dot(pt, hi) + _dot(pt, mid)) + _dot(pt, lo))
        outs.append(o_sub)
        lses.append(l_sub)
    ws = []
    for l_sub in lses:
        m = jnp.maximum(jnp.maximum(l_sub[0], l_sub[1]), l_sub[2])
        es = [jnp.exp(l - m) for l in l_sub]
        inv = 1.0 / (es[0] + es[1] + es[2])
        ws.append([e * inv for e in es])
    merged = []
    for o_sub, w_sub in zip(outs, ws):
        cols = []
        for h in heads:
            hc = slice(h * HEAD_DIM, (h + 1) * HEAD_DIM)
            acc = w_sub[0][:, h:h + 1] * o_sub[0][:, hc]
            acc += w_sub[1][:, h:h + 1] * o_sub[1][:, hc]
            acc += w_sub[2][:, h:h + 1] * o_sub[2][:, hc]
            cols.append(acc)
        merged.append(jnp.concatenate(cols, axis=1).astype(BF16))
    out_ref[...] = x_ref[...] + _dot(jnp.concatenate(merged, axis=0), wout_ref[...])


def _attn_merge(x, outs, lses, perms_t, w_out):
    s, d = x.shape
    n = w_out.shape[0]
    t = MERGE_ROWS
    d4, d16 = ATTN_PATTERNS[1][1], ATTN_PATTERNS[2][1]
    grouped = lambda dil, width: pl.BlockSpec((dil, t // dil, width), lambda i: (0, i, 0))
    return pl.pallas_call(
        _merge_kernel,
        grid=(s // t,),
        in_specs=[
            pl.BlockSpec((t, d), lambda i: (i, 0)),
            pl.BlockSpec((t, n), lambda i: (i, 0)),
            grouped(d4, n),
            grouped(d16, n),
            pl.BlockSpec((t, LANES), lambda i: (i, 0)),
            grouped(d4, LANES),
            grouped(d16, LANES),
            _resident((PERM_ROWS, PERM_ROWS), lambda i: (0, 0)),
            _resident((PERM_ROWS, PERM_ROWS), lambda i: (0, 0)),
            _resident((n, d), lambda i: (0, 0)),
        ],
        out_specs=pl.BlockSpec((t, d), lambda i: (i, 0)),
        out_shape=jax.ShapeDtypeStruct((s, d), F32),
        compiler_params=_params("parallel"),
        name="attn_merge",
    )(x, outs[0], outs[1], outs[2], lses[0], lses[1], lses[2],
      perms_t[0], perms_t[1], w_out)


def _attn_mixer(x, gain, w_qkv, attn_layer, w_out):
    s, d = x.shape
    n = ATTN_HEADS * HEAD_DIM
    perms = [_residue_permutation(PERM_ROWS, dil) for _, dil in ATTN_PATTERNS[1:]]
    p = [jnp.asarray(m, BF16) for m in perms]
    pt = [jnp.asarray(m.T, BF16) for m in perms]
    xn_groups = _norm_perm(x, gain, p)
    outs, lses = [], []
    for g, (_, dil) in enumerate(ATTN_PATTERNS):
        qkv = _qkv_proj(xn_groups[g].reshape(s, d), w_qkv, attn_layer, g)
        o, lse = _attention(qkv.reshape(dil, s // dil, 3 * n), dil)
        if dil == 1:
            o, lse = o.reshape(s, n), lse.reshape(s, LANES)
        outs.append(o)
        lses.append(lse)
    return _attn_merge(x, outs, lses, pt, w_out)


def kernel(x, norm_mix, pool_w_in, pool_w_group, pool_scale, pool_w_out, sgu_w_in, sgu_v_norm, sgu_w_s, sgu_b_s, sgu_w_out, attn_w_qkv, attn_w_out, norm_mlp, mlp_w_up, mlp_w_down, norm_final):
    batch, seq, d = x.shape
    depth = norm_mix.shape[0]
    n_mixers = 3
    bf = lambda w: w.astype(BF16)
    row = lambda v: v.reshape(1, -1)
    outs = []
    seqs = [x.reshape(seq, d)] if batch == 1 else [x[b] for b in range(batch)]
    for h in seqs:
        for i in range(depth):
            kind, j = i % n_mixers, i // n_mixers
            gain = row(norm_mix[i])
            if kind == 0:
                first = (mlp_w_up, mlp_w_down, 0) if i == 0 else None
                h, *first_bf = _pool_mixer(h, gain, bf(pool_w_in[j]), bf(pool_w_group[j]),
                                           row(pool_scale[j]), bf(pool_w_out[j]), first)
                if i == 0:
                    w_up, w_down = first_bf
            elif kind == 1:
                h = _sgu_mixer(h, gain, bf(sgu_w_in[j]), row(sgu_v_norm[j]), bf(sgu_w_s[j]),
                               sgu_b_s[j].T, bf(sgu_w_out[j]))
            else:
                h = _attn_mixer(h, gain, attn_w_qkv, j, bf(attn_w_out[j]))
            nxt = (mlp_w_up, mlp_w_down, i + 1) if i + 1 < depth else None
            h, *next_bf = _mlp(h, row(norm_mlp[i]), w_up, w_down, row(norm_final), nxt)
            if nxt is not None:
                w_up, w_down = next_bf
        outs.append(h)
    return outs[0].reshape(1, seq, d) if batch == 1 else jnp.stack(outs, axis=0)
```

```python
import functools

import jax
import jax.numpy as jnp
import numpy as np
from jax import lax
from jax.experimental import pallas as pl
from jax.experimental.pallas import tpu as pltpu

F32 = jnp.float32
BF16 = jnp.bfloat16

RMS_EPS = 1e-6
POOL_WINDOWS = (2, 4, 8, 16)
SGU_CHUNK = 128
SGU_GROUPS = 8
ATTN_PATTERNS = ((128, 1), (512, 4), (2048, 16))
ATTN_HEADS = 8
HEAD_DIM = 128
ATTN_BLOCK = 128
NEG_INF = -1e30

V7X_VMEM_LIMIT_BYTES = 62 * 1024 * 1024
LANES = 128

MLP_ROWS = 1024
MLP_FF_COLS = 1024
MIX_ROWS = 512
SGU_ROWS = 512
PERM_ROWS = 256
MERGE_ROWS = 512
PROJ_ROWS = 1024
ATTN_Q_ROWS = 512
ATTN_CHAIN_BATCH = 8
POOL_HALO = 32


def _params(*semantics):
    return pltpu.CompilerParams(dimension_semantics=semantics,
                                vmem_limit_bytes=V7X_VMEM_LIMIT_BYTES)


def _resident(shape, index_map):
    return pl.BlockSpec(shape, index_map, pipeline_mode=pl.Buffered(1))


def _rmsnorm(xf, gain_row):
    ms = jnp.mean(xf * xf, axis=-1, keepdims=True)
    return (xf * lax.rsqrt(ms + RMS_EPS)) * gain_row


def _dot(a, b):
    return lax.dot_general(a, b, (((1,), (0,)), ((), ())), preferred_element_type=F32)


def _mlp_weight_cast_specs(w_up, w_down, layer, steps, step_of):
    _, d, ff = w_up.shape
    in_specs = [pl.BlockSpec((None, d // steps, ff), lambda *g: (layer, step_of(*g), 0)),
                pl.BlockSpec((None, ff // steps, d), lambda *g: (layer, step_of(*g), 0))]
    out_specs = [pl.BlockSpec((d // steps, ff), lambda *g: (step_of(*g), 0)),
                 pl.BlockSpec((ff // steps, d), lambda *g: (step_of(*g), 0))]
    out_shape = [jax.ShapeDtypeStruct((d, ff), BF16), jax.ShapeDtypeStruct((ff, d), BF16)]
    return in_specs, out_specs, out_shape


def _cast_mlp_weights(wup_ref, wdn_ref, wup_bf_ref, wdn_bf_ref):
    wup_bf_ref[...] = wup_ref[...].astype(BF16)
    wdn_bf_ref[...] = wdn_ref[...].astype(BF16)


def _mlp_kernel(x_ref, g_ref, wup_ref, wdn_ref, gf_ref, *refs, last):
    if last:
        o_ref, xn_ref = refs
    else:
        nup_ref, ndn_ref, o_ref, nup_bf_ref, ndn_bf_ref, xn_ref = refs
    f = pl.program_id(1)

    @pl.when(f == 0)
    def _():
        x = x_ref[...]
        xn_ref[...] = _rmsnorm(x, g_ref[...]).astype(BF16)
        o_ref[...] = x

    h = jnp.maximum(_dot(xn_ref[...], wup_ref[...]), 0.0)
    o_ref[...] += _dot((h * h).astype(BF16), wdn_ref[...])

    if last:
        @pl.when(f == pl.num_programs(1) - 1)
        def _():
            o_ref[...] = _rmsnorm(o_ref[...], gf_ref[...])
    else:
        _cast_mlp_weights(nup_ref, ndn_ref, nup_bf_ref, ndn_bf_ref)


def _mlp(x, gain, w_up, w_down, final_gain, next_weights):
    s, d = x.shape
    ff = w_up.shape[1]
    tm, tf = MLP_ROWS, MLP_FF_COLS
    last = next_weights is None
    n_chunks = ff // tf
    in_specs = [
        pl.BlockSpec((tm, d), lambda i, f: (i, 0)),
        pl.BlockSpec((1, d), lambda i, f: (0, 0)),
        pl.BlockSpec((d, tf), lambda i, f: (0, f)),
        pl.BlockSpec((tf, d), lambda i, f: (f, 0)),
        pl.BlockSpec((1, d), lambda i, f: (0, 0)),
    ]
    out_specs = [pl.BlockSpec((tm, d), lambda i, f: (i, 0))]
    out_shape = [jax.ShapeDtypeStruct((s, d), F32)]
    operands = [x, gain, w_up, w_down, final_gain]
    if not last:
        cast_in, cast_out, cast_shape = _mlp_weight_cast_specs(
            *next_weights, (s // tm) * n_chunks, lambda i, f: i * n_chunks + f)
        in_specs += cast_in
        out_specs += cast_out
        out_shape += cast_shape
        operands += list(next_weights[:2])
    return pl.pallas_call(
        functools.partial(_mlp_kernel, last=last),
        grid=(s // tm, n_chunks),
        in_specs=in_specs,
        out_specs=out_specs,
        out_shape=out_shape,
        scratch_shapes=[pltpu.VMEM((tm, d), BF16)],
        compiler_params=_params("parallel", "arbitrary"),
        name="mlp_final" if last else "mlp",
    )(*operands)


def _pool_kernel(x_ref, g_ref, win_ref, wg_ref, sc_ref, wout_ref, *refs, cast):
    if cast:
        wup_ref, wdn_ref, o_ref, wup_bf_ref, wdn_bf_ref, hext_ref, t0_ref, t1_ref = refs
    else:
        o_ref, hext_ref, t0_ref, t1_ref = refs
    b = pl.program_id(0)
    tm, d = x_ref.shape
    halo = POOL_HALO
    gc = d // len(POOL_WINDOWS)

    @pl.when(b == 0)
    def _():
        hext_ref[0:halo, :] = jnp.zeros((halo, d), F32)
        t0_ref[0:halo, :] = jnp.zeros((halo, gc), F32)
        t1_ref[0:halo, :] = jnp.zeros((halo, gc), F32)

    @pl.when(b > 0)
    def _():
        hext_ref[halo - 16:halo, :] = hext_ref[tm + halo - 16:tm + halo, :]

    x = x_ref[...]
    xn = _rmsnorm(x, g_ref[...]).astype(BF16)
    hext_ref[halo:halo + tm, :] = _dot(xn, win_ref[...])

    pos = b * tm + lax.broadcasted_iota(jnp.int32, (tm, 1), 0)
    temps = (t0_ref, t1_ref)
    mixed = []
    for g, w in enumerate(POOL_WINDOWS):
        cols = slice(g * gc, (g + 1) * gc)
        src, src_cols, shift, levels = hext_ref, cols, 1, g + 1
        for lvl in range(levels):
            last = lvl == levels - 1
            lo = halo if last else halo - 16
            n = tm if last else tm + 16
            s = src[lo:lo + n, src_cols] + src[lo - shift:lo - shift + n, src_cols]
            if not last:
                dst = temps[lvl % 2]
                dst[lo:lo + n, :] = s
                src, src_cols = dst, slice(None)
            shift *= 2
        inv_count = 1.0 / jnp.minimum(pos + 1, w).astype(F32)
        pooled = s * inv_count - hext_ref[halo:halo + tm, cols]
        mixed.append(_dot(pooled.astype(BF16), wg_ref[g]))
    mixed = (jnp.concatenate(mixed, axis=1) * sc_ref[...]).astype(BF16)
    o_ref[...] = x + _dot(mixed, wout_ref[...])
    if cast:
        _cast_mlp_weights(wup_ref, wdn_ref, wup_bf_ref, wdn_bf_ref)


def _pool_mixer(x, gain, w_in, w_group, scale, w_out, cast_weights):
    s, d = x.shape
    tm = MIX_ROWS
    ng, gc, _ = w_group.shape
    steps = s // tm
    in_specs = [
        pl.BlockSpec((tm, d), lambda i: (i, 0)),
        _resident((1, d), lambda i: (0, 0)),
        _resident((d, d), lambda i: (0, 0)),
        _resident((ng, gc, gc), lambda i: (0, 0, 0)),
        _resident((1, d), lambda i: (0, 0)),
        _resident((d, d), lambda i: (0, 0)),
    ]
    out_specs = [pl.BlockSpec((tm, d), lambda i: (i, 0))]
    out_shape = [jax.ShapeDtypeStruct((s, d), F32)]
    operands = [x, gain, w_in, w_group, scale, w_out]
    cast = cast_weights is not None
    if cast:
        cast_in, cast_out, cast_shape = _mlp_weight_cast_specs(*cast_weights, steps, lambda i: i)
        in_specs += cast_in
        out_specs += cast_out
        out_shape += cast_shape
        operands += list(cast_weights[:2])
    return pl.pallas_call(
        functools.partial(_pool_kernel, cast=cast),
        grid=(steps,),
        in_specs=in_specs,
        out_specs=out_specs,
        out_shape=out_shape,
        scratch_shapes=[pltpu.VMEM((tm + POOL_HALO, d), F32),
                        pltpu.VMEM((tm + POOL_HALO, gc), F32),
                        pltpu.VMEM((tm + POOL_HALO, gc), F32)],
        compiler_params=_params("arbitrary"),
        name="pool_mixer",
    )(*operands)


def _gelu(x):
    return 0.5 * x * (1.0 + lax.erf(x * np.float32(np.sqrt(0.5))))


def _sgu_kernel(x_ref, g_ref, win_ref, vn_ref, ws_ref, bs_ref, wout_ref, o_ref,
                u_ref, v_ref, gated_ref):
    tm, d = x_ref.shape
    e = win_ref.shape[1] // 2
    gcols = e // SGU_GROUPS
    x = x_ref[...]
    xn = _rmsnorm(x, g_ref[...]).astype(BF16)
    u_ref[...] = _gelu(_dot(xn, win_ref[:, :e]))
    v = _gelu(_dot(xn, win_ref[:, e:]))
    v_ref[...] = _rmsnorm(v, vn_ref[...]).astype(BF16)

    t = lax.broadcasted_iota(jnp.int32, (SGU_CHUNK, SGU_CHUNK), 0)
    sidx = lax.broadcasted_iota(jnp.int32, (SGU_CHUNK, SGU_CHUNK), 1)
    causal = sidx <= t
    for g in range(SGU_GROUPS):
        ws = jnp.where(causal, ws_ref[g], jnp.zeros_like(ws_ref[g]))
        bias = bs_ref[:, g:g + 1]
        cols = slice(g * gcols, (g + 1) * gcols)
        for c in range(tm // SGU_CHUNK):
            rows = slice(c * SGU_CHUNK, (c + 1) * SGU_CHUNK)
            sp = _dot(ws, v_ref[rows, cols]) + bias
            gated_ref[rows, cols] = (u_ref[rows, cols] * sp).astype(BF16)
    o_ref[...] = x + _dot(gated_ref[...], wout_ref[...])


def _sgu_mixer(x, gain, w_in, v_norm, w_s, b_s_t, w_out):
    s, d = x.shape
    tm = SGU_ROWS
    e = w_out.shape[0]
    return pl.pallas_call(
        _sgu_kernel,
        grid=(s // tm,),
        in_specs=[
            pl.BlockSpec((tm, d), lambda i: (i, 0)),
            _resident((1, d), lambda i: (0, 0)),
            _resident((d, 2 * e), lambda i: (0, 0)),
            _resident((1, e), lambda i: (0, 0)),
            _resident(w_s.shape, lambda i: (0, 0, 0)),
            _resident(b_s_t.shape, lambda i: (0, 0)),
            _resident((e, d), lambda i: (0, 0)),
        ],
        out_specs=pl.BlockSpec((tm, d), lambda i: (i, 0)),
        out_shape=jax.ShapeDtypeStruct((s, d), F32),
        scratch_shapes=[pltpu.VMEM((tm, e), F32),
                        pltpu.VMEM((tm, e), BF16),
                        pltpu.VMEM((tm, e), BF16)],
        compiler_params=_params("parallel"),
        name="sgu_mixer",
    )(x, gain, w_in, v_norm, w_s, b_s_t, w_out)


def _residue_permutation(rows, dilation):
    n = np.arange(rows)
    per = rows // dilation
    old = dilation * (n % per) + n // per
    p = np.zeros((rows, rows), np.float32)
    p[n, old] = 1.0
    return p


def _qkv_kernel(x_ref, g_ref, p4_ref, p16_ref, w_ref, o1_ref, o4_ref, o16_ref, xn_ref, a_ref):
    j = pl.program_id(1)
    tm = x_ref.shape[0]
    pr = PERM_ROWS

    @pl.when(j == 0)
    def _():
        xn_ref[...] = _rmsnorm(x_ref[...], g_ref[...]).astype(BF16)

    for g, p_ref in ((1, p4_ref), (2, p16_ref)):
        dil = ATTN_PATTERNS[g][1]

        @pl.when(j == 3 * g)
        def _(p_ref=p_ref, dil=dil):
            per = pr // dil
            for sub in range(tm // pr):
                y = _dot(p_ref[...], xn_ref[sub * pr:(sub + 1) * pr, :]).astype(BF16)
                for r in range(dil):
                    dst = r * (tm // dil) + sub * per
                    a_ref[dst:dst + per, :] = y[r * per:(r + 1) * per, :]

    @pl.when(j < 3)
    def _():
        o1_ref[...] = _dot(xn_ref[...], w_ref[...]).astype(BF16)

    for g, o_ref in ((1, o4_ref), (2, o16_ref)):
        @pl.when(jnp.logical_and(j >= 3 * g, j < 3 * g + 3))
        def _(o_ref=o_ref):
            o_ref[...] = _dot(a_ref[...], w_ref[...]).astype(BF16).reshape(o_ref.shape)


def _qkv_proj(x, gain, perms, w_qkv, layer):
    s, d = x.shape
    n = ATTN_HEADS * HEAD_DIM
    n_groups = len(ATTN_PATTERNS)
    tm = PROJ_ROWS
    d4, d16 = ATTN_PATTERNS[1][1], ATTN_PATTERNS[2][1]
    part = lambda j, g: jnp.clip(j - 3 * g, 0, 2)
    return pl.pallas_call(
        _qkv_kernel,
        grid=(s // tm, 3 * n_groups),
        in_specs=[
            pl.BlockSpec((tm, d), lambda i, j: (i, 0)),
            _resident((1, d), lambda i, j: (0, 0)),
            _resident((PERM_ROWS, PERM_ROWS), lambda i, j: (0, 0)),
            _resident((PERM_ROWS, PERM_ROWS), lambda i, j: (0, 0)),
            pl.BlockSpec((None, d, n), lambda i, j: (layer, 0, (j % 3) * n_groups + j // 3)),
        ],
        out_specs=[
            pl.BlockSpec((tm, n), lambda i, j: (i, part(j, 0))),
            pl.BlockSpec((d4, tm // d4, n), lambda i, j: (0, i, part(j, 1))),
            pl.BlockSpec((d16, tm // d16, n), lambda i, j: (0, i, part(j, 2))),
        ],
        out_shape=[
            jax.ShapeDtypeStruct((s, 3 * n), BF16),
            jax.ShapeDtypeStruct((d4, s // d4, 3 * n), BF16),
            jax.ShapeDtypeStruct((d16, s // d16, 3 * n), BF16),
        ],
        scratch_shapes=[pltpu.VMEM((tm, d), BF16), pltpu.VMEM((tm, d), BF16)],
        compiler_params=_params("parallel", "arbitrary"),
        name="attn_qkv_proj",
    )(x, gain, perms[0], perms[1], w_qkv)


def _attn_kernel(q_ref, kp_ref, kc_ref, vp_ref, vc_ref, o_ref, lse_ref):
    ib = pl.program_id(1)
    tq = q_ref.shape[1]
    blk = ATTN_BLOCK
    scale = HEAD_DIM ** -0.5
    heads = range(ATTN_HEADS)
    row = lax.broadcasted_iota(jnp.int32, (blk, 2 * blk), 0)
    col = lax.broadcasted_iota(jnp.int32, (blk, 2 * blk), 1)
    band = jnp.logical_and(col >= row, col <= row + blk)
    first_band = jnp.logical_and(band, jnp.logical_or(col >= blk, ib > 0))
    lane = lax.broadcasted_iota(jnp.int32, (blk, LANES), 1)
    contract_last = (((1,), (1,)), ((), ()))

    def window(prev_ref, cur_ref, sub, hc):
        if sub == 0:
            return jnp.concatenate([prev_ref[0, :, hc], cur_ref[0, 0:blk, hc]], axis=0)
        return cur_ref[0, (sub - 1) * blk:(sub + 1) * blk, hc]

    chains = [(sub, h) for sub in range(tq // blk) for h in range(ATTN_HEADS)]
    lse_rows = {}
    for c0 in range(0, len(chains), ATTN_CHAIN_BATCH):
        batch = chains[c0:c0 + ATTN_CHAIN_BATCH]
        rows = [slice(sub * blk, (sub + 1) * blk) for sub, _ in batch]
        hcs = [slice(h * HEAD_DIM, (h + 1) * HEAD_DIM) for _, h in batch]
        s = [lax.dot_general(q_ref[0, r, hc], window(kp_ref, kc_ref, sub, hc), contract_last,
                             preferred_element_type=F32)
             for (sub, _), r, hc in zip(batch, rows, hcs)]
        s = [jnp.where(first_band if sub == 0 else band, sh * scale, NEG_INF)
             for (sub, _), sh in zip(batch, s)]
        m = [jnp.max(sh, axis=-1, keepdims=True) for sh in s]
        p = [jnp.exp(sh - mh) for sh, mh in zip(s, m)]
        den = [jnp.sum(ph, axis=-1, keepdims=True) for ph in p]
        o = [_dot(ph.astype(BF16), window(vp_ref, vc_ref, sub, hc))
             for (sub, _), ph, hc in zip(batch, p, hcs)]
        for k, (sub, h) in enumerate(batch):
            o_ref[0, rows[k], hcs[k]] = (o[k] / den[k]).astype(o_ref.dtype)
            acc = lse_rows.get(sub, jnp.zeros((blk, LANES), F32))
            lse_rows[sub] = jnp.where(lane == h, m[k] + jnp.log(den[k]), acc)
    for sub, v in lse_rows.items():
        lse_ref[0, sub * blk:(sub + 1) * blk, :] = v


def _attention(qkv, dilation):
    d, length, n3 = qkv.shape
    n = n3 // 3
    tq = ATTN_Q_ROWS
    sub_per_tile = tq // ATTN_BLOCK
    prev_map = lambda col: (lambda r, i: (r, jnp.maximum(i * sub_per_tile - 1, 0), col))
    cur_map = lambda col: (lambda r, i: (r, i, col))
    return pl.pallas_call(
        _attn_kernel,
        grid=(d, length // tq),
        in_specs=[
            pl.BlockSpec((1, tq, n), cur_map(0)),
            pl.BlockSpec((1, ATTN_BLOCK, n), prev_map(1)),
            pl.BlockSpec((1, tq, n), cur_map(1)),
            pl.BlockSpec((1, ATTN_BLOCK, n), prev_map(2)),
            pl.BlockSpec((1, tq, n), cur_map(2)),
        ],
        out_specs=[
            pl.BlockSpec((1, tq, n), cur_map(0)),
            pl.BlockSpec((1, tq, LANES), cur_map(0)),
        ],
        out_shape=[
            jax.ShapeDtypeStruct((d, length, n), BF16),
            jax.ShapeDtypeStruct((d, length, LANES), F32),
        ],
        compiler_params=_params("parallel", "parallel"),
        name=f"window_attn_d{dilation}",
    )(qkv, qkv, qkv, qkv, qkv)


def _split3(a):
    hi = a.astype(BF16)
    r1 = a - hi.astype(F32)
    mid = r1.astype(BF16)
    lo = (r1 - mid.astype(F32)).astype(BF16)
    return hi, mid, lo


def _merge_kernel(x_ref, o1_ref, o4_ref, o16_ref, l1_ref, l4_ref, l16_ref,
                  p4t_ref, p16t_ref, wout_ref, out_ref):
    t, n = o1_ref.shape
    pr = PERM_ROWS
    subs = range(t // pr)
    heads = range(ATTN_HEADS)

    def regrouped_rows(ref, dil, sub):
        per = pr // dil
        return ref[:, sub * per:(sub + 1) * per, :].reshape(pr, ref.shape[2])

    outs, lses = [], []
    for sub in subs:
        rows = slice(sub * pr, (sub + 1) * pr)
        o_sub, l_sub = [o1_ref[rows, :].astype(F32)], [l1_ref[rows, :]]
        for o_ref, l_ref, pt_ref, dil in ((o4_ref, l4_ref, p4t_ref, ATTN_PATTERNS[1][1]),
                                          (o16_ref, l16_ref, p16t_ref, ATTN_PATTERNS[2][1])):
            pt = pt_ref[...]
            o_sub.append(_dot(pt, regrouped_rows(o_ref, dil, sub)))
            hi, mid, lo = _split3(regrouped_rows(l_ref, dil, sub))
            l_sub.append((_dot(pt, hi) + _dot(pt, mid)) + _dot(pt, lo))
        outs.append(o_sub)
        lses.append(l_sub)
    ws = []
    for l_sub in lses:
        m = jnp.maximum(jnp.maximum(l_sub[0], l_sub[1]), l_sub[2])
        es = [jnp.exp(l - m) for l in l_sub]
        inv = 1.0 / (es[0] + es[1] + es[2])
        ws.append([e * inv for e in es])
    merged = []
    for o_sub, w_sub in zip(outs, ws):
        cols = []
        for h in heads:
            hc = slice(h * HEAD_DIM, (h + 1) * HEAD_DIM)
            acc = w_sub[0][:, h:h + 1] * o_sub[0][:, hc]
            acc += w_sub[1][:, h:h + 1] * o_sub[1][:, hc]
            acc += w_sub[2][:, h:h + 1] * o_sub[2][:, hc]
            cols.append(acc)
        merged.append(jnp.concatenate(cols, axis=1).astype(BF16))
    out_ref[...] = x_ref[...] + _dot(jnp.concatenate(merged, axis=0), wout_ref[...])


def _attn_merge(x, outs, lses, perms_t, w_out):
    s, d = x.shape
    n = w_out.shape[0]
    t = MERGE_ROWS
    d4, d16 = ATTN_PATTERNS[1][1], ATTN_PATTERNS[2][1]
    grouped = lambda dil, width: pl.BlockSpec((dil, t // dil, width), lambda i: (0, i, 0))
    return pl.pallas_call(
        _merge_kernel,
        grid=(s // t,),
        in_specs=[
            pl.BlockSpec((t, d), lambda i: (i, 0)),
            pl.BlockSpec((t, n), lambda i: (i, 0)),
            grouped(d4, n),
            grouped(d16, n),
            pl.BlockSpec((t, LANES), lambda i: (i, 0)),
            grouped(d4, LANES),
            grouped(d16, LANES),
            _resident((PERM_ROWS, PERM_ROWS), lambda i: (0, 0)),
            _resident((PERM_ROWS, PERM_ROWS), lambda i: (0, 0)),
            _resident((n, d), lambda i: (0, 0)),
        ],
        out_specs=pl.BlockSpec((t, d), lambda i: (i, 0)),
        out_shape=jax.ShapeDtypeStruct((s, d), F32),
        compiler_params=_params("parallel"),
        name="attn_merge",
    )(x, outs[0], outs[1], outs[2], lses[0], lses[1], lses[2],
      perms_t[0], perms_t[1], w_out)


def _attn_mixer(x, gain, w_qkv, attn_layer, w_out):
    s, d = x.shape
    n = ATTN_HEADS * HEAD_DIM
    perms = [_residue_permutation(PERM_ROWS, dil) for _, dil in ATTN_PATTERNS[1:]]
    p = [jnp.asarray(m, BF16) for m in perms]
    pt = [jnp.asarray(m.T, BF16) for m in perms]
    qkvs = _qkv_proj(x, gain, p, w_qkv, attn_layer)
    outs, lses = [], []
    for (_, dil), qkv in zip(ATTN_PATTERNS, qkvs):
        o, lse = _attention(qkv.reshape(dil, s // dil, 3 * n), dil)
        if dil == 1:
            o, lse = o.reshape(s, n), lse.reshape(s, LANES)
        outs.append(o)
        lses.append(lse)
    return _attn_merge(x, outs, lses, pt, w_out)


def kernel(x, norm_mix, pool_w_in, pool_w_group, pool_scale, pool_w_out, sgu_w_in, sgu_v_norm, sgu_w_s, sgu_b_s, sgu_w_out, attn_w_qkv, attn_w_out, norm_mlp, mlp_w_up, mlp_w_down, norm_final):
    batch, seq, d = x.shape
    depth = norm_mix.shape[0]
    n_mixers = 3
    bf = lambda w: w.astype(BF16)
    row = lambda v: v.reshape(1, -1)
    outs = []
    seqs = [x.reshape(seq, d)] if batch == 1 else [x[b] for b in range(batch)]
    for h in seqs:
        for i in range(depth):
            kind, j = i % n_mixers, i // n_mixers
            gain = row(norm_mix[i])
            if kind == 0:
                first = (mlp_w_up, mlp_w_down, 0) if i == 0 else None
                h, *first_bf = _pool_mixer(h, gain, bf(pool_w_in[j]), bf(pool_w_group[j]),
                                           row(pool_scale[j]), bf(pool_w_out[j]), first)
                if i == 0:
                    w_up, w_down = first_bf
            elif kind == 1:
                h = _sgu_mixer(h, gain, bf(sgu_w_in[j]), row(sgu_v_norm[j]), bf(sgu_w_s[j]),
                               sgu_b_s[j].T, bf(sgu_w_out[j]))
            else:
                h = _attn_mixer(h, gain, attn_w_qkv, j, bf(attn_w_out[j]))
            nxt = (mlp_w_up, mlp_w_down, i + 1) if i + 1 < depth else None
            h, *next_bf = _mlp(h, row(norm_mlp[i]), w_up, w_down, row(norm_final), nxt)
            if nxt is not None:
                w_up, w_down = next_bf
        outs.append(h)
    return outs[0].reshape(1, seq, d) if batch == 1 else jnp.stack(outs, axis=0)
```

```python
import functools

import jax
import jax.numpy as jnp
import numpy as np
from jax import lax
from jax.experimental import pallas as pl
from jax.experimental.pallas import tpu as pltpu

F32 = jnp.float32
BF16 = jnp.bfloat16

RMS_EPS = 1e-6
POOL_WINDOWS = (2, 4, 8, 16)
SGU_CHUNK = 128
SGU_GROUPS = 8
ATTN_PATTERNS = ((128, 1), (512, 4), (2048, 16))
ATTN_HEADS = 8
HEAD_DIM = 128
ATTN_BLOCK = 128
NEG_INF = -1e30

V7X_VMEM_LIMIT_BYTES = 62 * 1024 * 1024
LANES = 128

MLP_ROWS = 1024
MLP_FF_COLS = 1024
MIX_ROWS = 512
SGU_ROWS = 512
PERM_ROWS = 256
MERGE_ROWS = 512
NORM_PERM_ROWS = 1024
PROJ_ROWS = 2048
ATTN_Q_ROWS = 1024
ATTN_CHAIN_BATCH = 8
POOL_HALO = 32


def _params(*semantics):
    return pltpu.CompilerParams(dimension_semantics=semantics,
                                vmem_limit_bytes=V7X_VMEM_LIMIT_BYTES)


def _resident(shape, index_map):
    return pl.BlockSpec(shape, index_map, pipeline_mode=pl.Buffered(1))


def _rmsnorm(xf, gain_row):
    ms = jnp.mean(xf * xf, axis=-1, keepdims=True)
    return (xf * lax.rsqrt(ms + RMS_EPS)) * gain_row


def _dot(a, b):
    return lax.dot_general(a, b, (((1,), (0,)), ((), ())), preferred_element_type=F32)


def _mlp_weight_cast_specs(w_up, w_down, layer, steps, step_of):
    _, d, ff = w_up.shape
    in_specs = [pl.BlockSpec((None, d // steps, ff), lambda *g: (layer, step_of(*g), 0)),
                pl.BlockSpec((None, ff // steps, d), lambda *g: (layer, step_of(*g), 0))]
    out_specs = [pl.BlockSpec((d // steps, ff), lambda *g: (step_of(*g), 0)),
                 pl.BlockSpec((ff // steps, d), lambda *g: (step_of(*g), 0))]
    out_shape = [jax.ShapeDtypeStruct((d, ff), BF16), jax.ShapeDtypeStruct((ff, d), BF16)]
    return in_specs, out_specs, out_shape


def _cast_mlp_weights(wup_ref, wdn_ref, wup_bf_ref, wdn_bf_ref):
    wup_bf_ref[...] = wup_ref[...].astype(BF16)
    wdn_bf_ref[...] = wdn_ref[...].astype(BF16)


def _mlp_kernel(x_ref, g_ref, wup_ref, wdn_ref, gf_ref, *refs, last):
    if last:
        o_ref, xn_ref = refs
    else:
        nup_ref, ndn_ref, o_ref, nup_bf_ref, ndn_bf_ref, xn_ref = refs
    f = pl.program_id(1)

    @pl.when(f == 0)
    def _():
        x = x_ref[...]
        xn_ref[...] = _rmsnorm(x, g_ref[...]).astype(BF16)
        o_ref[...] = x

    h = jnp.maximum(_dot(xn_ref[...], wup_ref[...]), 0.0)
    o_ref[...] += _dot((h * h).astype(BF16), wdn_ref[...])

    if last:
        @pl.when(f == pl.num_programs(1) - 1)
        def _():
            o_ref[...] = _rmsnorm(o_ref[...], gf_ref[...])
    else:
        _cast_mlp_weights(nup_ref, ndn_ref, nup_bf_ref, ndn_bf_ref)


def _mlp(x, gain, w_up, w_down, final_gain, next_weights):
    s, d = x.shape
    ff = w_up.shape[1]
    tm, tf = MLP_ROWS, MLP_FF_COLS
    last = next_weights is None
    n_chunks = ff // tf
    in_specs = [
        pl.BlockSpec((tm, d), lambda i, f: (i, 0)),
        pl.BlockSpec((1, d), lambda i, f: (0, 0)),
        pl.BlockSpec((d, tf), lambda i, f: (0, f)),
        pl.BlockSpec((tf, d), lambda i, f: (f, 0)),
        pl.BlockSpec((1, d), lambda i, f: (0, 0)),
    ]
    out_specs = [pl.BlockSpec((tm, d), lambda i, f: (i, 0))]
    out_shape = [jax.ShapeDtypeStruct((s, d), F32)]
    operands = [x, gain, w_up, w_down, final_gain]
    if not last:
        cast_in, cast_out, cast_shape = _mlp_weight_cast_specs(
            *next_weights, (s // tm) * n_chunks, lambda i, f: i * n_chunks + f)
        in_specs += cast_in
        out_specs += cast_out
        out_shape += cast_shape
        operands += list(next_weights[:2])
    return pl.pallas_call(
        functools.partial(_mlp_kernel, last=last),
        grid=(s // tm, n_chunks),
        in_specs=in_specs,
        out_specs=out_specs,
        out_shape=out_shape,
        scratch_shapes=[pltpu.VMEM((tm, d), BF16)],
        compiler_params=_params("parallel", "arbitrary"),
        name="mlp_final" if last else "mlp",
    )(*operands)


def _pool_kernel(x_ref, g_ref, win_ref, wg_ref, sc_ref, wout_ref, *refs, cast):
    if cast:
        wup_ref, wdn_ref, o_ref, wup_bf_ref, wdn_bf_ref, hext_ref, t0_ref, t1_ref = refs
    else:
        o_ref, hext_ref, t0_ref, t1_ref = refs
    b = pl.program_id(0)
    tm, d = x_ref.shape
    halo = POOL_HALO
    gc = d // len(POOL_WINDOWS)

    @pl.when(b == 0)
    def _():
        hext_ref[0:halo, :] = jnp.zeros((halo, d), F32)
        t0_ref[0:halo, :] = jnp.zeros((halo, gc), F32)
        t1_ref[0:halo, :] = jnp.zeros((halo, gc), F32)

    @pl.when(b > 0)
    def _():
        hext_ref[halo - 16:halo, :] = hext_ref[tm + halo - 16:tm + halo, :]

    x = x_ref[...]
    xn = _rmsnorm(x, g_ref[...]).astype(BF16)
    hext_ref[halo:halo + tm, :] = _dot(xn, win_ref[...])

    pos = b * tm + lax.broadcasted_iota(jnp.int32, (tm, 1), 0)
    temps = (t0_ref, t1_ref)
    mixed = []
    for g, w in enumerate(POOL_WINDOWS):
        cols = slice(g * gc, (g + 1) * gc)
        src, src_cols, shift, levels = hext_ref, cols, 1, g + 1
        for lvl in range(levels):
            last = lvl == levels - 1
            lo = halo if last else halo - 16
            n = tm if last else tm + 16
            s = src[lo:lo + n, src_cols] + src[lo - shift:lo - shift + n, src_cols]
            if not last:
                dst = temps[lvl % 2]
                dst[lo:lo + n, :] = s
                src, src_cols = dst, slice(None)
            shift *= 2
        inv_count = 1.0 / jnp.minimum(pos + 1, w).astype(F32)
        pooled = s * inv_count - hext_ref[halo:halo + tm, cols]
        mixed.append(_dot(pooled.astype(BF16), wg_ref[g]))
    mixed = (jnp.concatenate(mixed, axis=1) * sc_ref[...]).astype(BF16)
    o_ref[...] = x + _dot(mixed, wout_ref[...])
    if cast:
        _cast_mlp_weights(wup_ref, wdn_ref, wup_bf_ref, wdn_bf_ref)


def _pool_mixer(x, gain, w_in, w_group, scale, w_out, cast_weights):
    s, d = x.shape
    tm = MIX_ROWS
    ng, gc, _ = w_group.shape
    steps = s // tm
    in_specs = [
        pl.BlockSpec((tm, d), lambda i: (i, 0)),
        _resident((1, d), lambda i: (0, 0)),
        _resident((d, d), lambda i: (0, 0)),
        _resident((ng, gc, gc), lambda i: (0, 0, 0)),
        _resident((1, d), lambda i: (0, 0)),
        _resident((d, d), lambda i: (0, 0)),
    ]
    out_specs = [pl.BlockSpec((tm, d), lambda i: (i, 0))]
    out_shape = [jax.ShapeDtypeStruct((s, d), F32)]
    operands = [x, gain, w_in, w_group, scale, w_out]
    cast = cast_weights is not None
    if cast:
        cast_in, cast_out, cast_shape = _mlp_weight_cast_specs(*cast_weights, steps, lambda i: i)
        in_specs += cast_in
        out_specs += cast_out
        out_shape += cast_shape
        operands += list(cast_weights[:2])
    return pl.pallas_call(
        functools.partial(_pool_kernel, cast=cast),
        grid=(steps,),
        in_specs=in_specs,
        out_specs=out_specs,
        out_shape=out_shape,
        scratch_shapes=[pltpu.VMEM((tm + POOL_HALO, d), F32),
                        pltpu.VMEM((tm + POOL_HALO, gc), F32),
                        pltpu.VMEM((tm + POOL_HALO, gc), F32)],
        compiler_params=_params("arbitrary"),
        name="pool_mixer",
    )(*operands)


def _gelu(x):
    return 0.5 * x * (1.0 + lax.erf(x * np.float32(np.sqrt(0.5))))


def _sgu_kernel(x_ref, g_ref, win_ref, vn_ref, ws_ref, bs_ref, wout_ref, o_ref,
                u_ref, v_ref, gated_ref):
    tm, d = x_ref.shape
    e = win_ref.shape[1] // 2
    gcols = e // SGU_GROUPS
    x = x_ref[...]
    xn = _rmsnorm(x, g_ref[...]).astype(BF16)
    u_ref[...] = _gelu(_dot(xn, win_ref[:, :e]))
    v = _gelu(_dot(xn, win_ref[:, e:]))
    v_ref[...] = _rmsnorm(v, vn_ref[...]).astype(BF16)

    t = lax.broadcasted_iota(jnp.int32, (SGU_CHUNK, SGU_CHUNK), 0)
    sidx = lax.broadcasted_iota(jnp.int32, (SGU_CHUNK, SGU_CHUNK), 1)
    causal = sidx <= t
    for g in range(SGU_GROUPS):
        ws = jnp.where(causal, ws_ref[g], jnp.zeros_like(ws_ref[g]))
        bias = bs_ref[:, g:g + 1]
        cols = slice(g * gcols, (g + 1) * gcols)
        for c in range(tm // SGU_CHUNK):
            rows = slice(c * SGU_CHUNK, (c + 1) * SGU_CHUNK)
            sp = _dot(ws, v_ref[rows, cols]) + bias
            gated_ref[rows, cols] = (u_ref[rows, cols] * sp).astype(BF16)
    o_ref[...] = x + _dot(gated_ref[...], wout_ref[...])


def _sgu_mixer(x, gain, w_in, v_norm, w_s, b_s_t, w_out):
    s, d = x.shape
    tm = SGU_ROWS
    e = w_out.shape[0]
    return pl.pallas_call(
        _sgu_kernel,
        grid=(s // tm,),
        in_specs=[
            pl.BlockSpec((tm, d), lambda i: (i, 0)),
            _resident((1, d), lambda i: (0, 0)),
            _resident((d, 2 * e), lambda i: (0, 0)),
            _resident((1, e), lambda i: (0, 0)),
            _resident(w_s.shape, lambda i: (0, 0, 0)),
            _resident(b_s_t.shape, lambda i: (0, 0)),
            _resident((e, d), lambda i: (0, 0)),
        ],
        out_specs=pl.BlockSpec((tm, d), lambda i: (i, 0)),
        out_shape=jax.ShapeDtypeStruct((s, d), F32),
        scratch_shapes=[pltpu.VMEM((tm, e), F32),
                        pltpu.VMEM((tm, e), BF16),
                        pltpu.VMEM((tm, e), BF16)],
        compiler_params=_params("parallel"),
        name="sgu_mixer",
    )(x, gain, w_in, v_norm, w_s, b_s_t, w_out)


def _residue_permutation(rows, dilation):
    n = np.arange(rows)
    per = rows // dilation
    old = dilation * (n % per) + n // per
    p = np.zeros((rows, rows), np.float32)
    p[n, old] = 1.0
    return p


def _norm_perm_kernel(x_ref, g_ref, p4_ref, p16_ref, o1_ref, o4_ref, o16_ref):
    t = x_ref.shape[0]
    pr = PERM_ROWS
    xn = _rmsnorm(x_ref[...], g_ref[...]).astype(BF16)
    o1_ref[...] = xn
    for p_ref, o_ref in ((p4_ref, o4_ref), (p16_ref, o16_ref)):
        dil = o_ref.shape[0]
        per = pr // dil
        for sub in range(t // pr):
            y = _dot(p_ref[...], xn[sub * pr:(sub + 1) * pr, :]).astype(BF16)
            o_ref[:, sub * per:(sub + 1) * per, :] = y.reshape(dil, per, y.shape[1])


def _norm_perm(x, gain, perms):
    s, d = x.shape
    t = NORM_PERM_ROWS
    d4, d16 = ATTN_PATTERNS[1][1], ATTN_PATTERNS[2][1]
    return pl.pallas_call(
        _norm_perm_kernel,
        grid=(s // t,),
        in_specs=[
            pl.BlockSpec((t, d), lambda i: (i, 0)),
            _resident((1, d), lambda i: (0, 0)),
            _resident((PERM_ROWS, PERM_ROWS), lambda i: (0, 0)),
            _resident((PERM_ROWS, PERM_ROWS), lambda i: (0, 0)),
        ],
        out_specs=[
            pl.BlockSpec((t, d), lambda i: (i, 0)),
            pl.BlockSpec((d4, t // d4, d), lambda i: (0, i, 0)),
            pl.BlockSpec((d16, t // d16, d), lambda i: (0, i, 0)),
        ],
        out_shape=[
            jax.ShapeDtypeStruct((s, d), BF16),
            jax.ShapeDtypeStruct((d4, s // d4, d), BF16),
            jax.ShapeDtypeStruct((d16, s // d16, d), BF16),
        ],
        compiler_params=_params("parallel"),
        name="attn_norm_perm",
    )(x, gain, perms[0], perms[1])


def _proj_kernel(a_ref, w_ref, o_ref):
    o_ref[...] = _dot(a_ref[...], w_ref[...]).astype(o_ref.dtype)


def _qkv_proj(a, w_qkv, layer, group):
    s, d = a.shape
    n = ATTN_HEADS * HEAD_DIM
    n_groups = len(ATTN_PATTERNS)
    tm = PROJ_ROWS
    return pl.pallas_call(
        _proj_kernel,
        grid=(3, s // tm),
        in_specs=[
            pl.BlockSpec((tm, d), lambda j, i: (i, 0)),
            pl.BlockSpec((None, d, n), lambda j, i: (layer, 0, j * n_groups + group)),
        ],
        out_specs=pl.BlockSpec((tm, n), lambda j, i: (i, j)),
        out_shape=jax.ShapeDtypeStruct((s, 3 * n), BF16),
        compiler_params=_params("parallel", "parallel"),
        name=f"qkv_proj_g{group}",
    )(a, w_qkv)


def _attn_kernel(q_ref, kp_ref, kc_ref, vp_ref, vc_ref, o_ref, lse_ref):
    ib = pl.program_id(1)
    tq = q_ref.shape[1]
    blk = ATTN_BLOCK
    scale = HEAD_DIM ** -0.5
    heads = range(ATTN_HEADS)
    row = lax.broadcasted_iota(jnp.int32, (blk, 2 * blk), 0)
    col = lax.broadcasted_iota(jnp.int32, (blk, 2 * blk), 1)
    band = jnp.logical_and(col >= row, col <= row + blk)
    first_band = jnp.logical_and(band, jnp.logical_or(col >= blk, ib > 0))
    lane = lax.broadcasted_iota(jnp.int32, (blk, LANES), 1)
    contract_last = (((1,), (1,)), ((), ()))

    def window(prev_ref, cur_ref, sub, hc):
        if sub == 0:
            return jnp.concatenate([prev_ref[0, :, hc], cur_ref[0, 0:blk, hc]], axis=0)
        return cur_ref[0, (sub - 1) * blk:(sub + 1) * blk, hc]

    chains = [(sub, h) for sub in range(tq // blk) for h in range(ATTN_HEADS)]
    lse_rows = {}
    for c0 in range(0, len(chains), ATTN_CHAIN_BATCH):
        batch = chains[c0:c0 + ATTN_CHAIN_BATCH]
        rows = [slice(sub * blk, (sub + 1) * blk) for sub, _ in batch]
        hcs = [slice(h * HEAD_DIM, (h + 1) * HEAD_DIM) for _, h in batch]
        s = [lax.dot_general(q_ref[0, r, hc], window(kp_ref, kc_ref, sub, hc), contract_last,
                             preferred_element_type=F32)
             for (sub, _), r, hc in zip(batch, rows, hcs)]
        s = [jnp.where(first_band if sub == 0 else band, sh * scale, NEG_INF)
             for (sub, _), sh in zip(batch, s)]
        m = [jnp.max(sh, axis=-1, keepdims=True) for sh in s]
        p = [jnp.exp(sh - mh) for sh, mh in zip(s, m)]
        den = [jnp.sum(ph, axis=-1, keepdims=True) for ph in p]
        o = [_dot(ph.astype(BF16), window(vp_ref, vc_ref, sub, hc))
             for (sub, _), ph, hc in zip(batch, p, hcs)]
        for k, (sub, h) in enumerate(batch):
            o_ref[0, rows[k], hcs[k]] = (o[k] / den[k]).astype(o_ref.dtype)
            acc = lse_rows.get(sub, jnp.zeros((blk, LANES), F32))
            lse_rows[sub] = jnp.where(lane == h, m[k] + jnp.log(den[k]), acc)
    for sub, v in lse_rows.items():
        lse_ref[0, sub * blk:(sub + 1) * blk, :] = v


def _attention(qkv, dilation):
    d, length, n3 = qkv.shape
    n = n3 // 3
    tq = ATTN_Q_ROWS
    sub_per_tile = tq // ATTN_BLOCK
    prev_map = lambda col: (lambda r, i: (r, jnp.maximum(i * sub_per_tile - 1, 0), col))
    cur_map = lambda col: (lambda r, i: (r, i, col))
    return pl.pallas_call(
        _attn_kernel,
        grid=(d, length // tq),
        in_specs=[
            pl.BlockSpec((1, tq, n), cur_map(0)),
            pl.BlockSpec((1, ATTN_BLOCK, n), prev_map(1)),
            pl.BlockSpec((1, tq, n), cur_map(1)),
            pl.BlockSpec((1, ATTN_BLOCK, n), prev_map(2)),
            pl.BlockSpec((1, tq, n), cur_map(2)),
        ],
        out_specs=[
            pl.BlockSpec((1, tq, n), cur_map(0)),
            pl.BlockSpec((1, tq, LANES), cur_map(0)),
        ],
        out_shape=[
            jax.ShapeDtypeStruct((d, length, n), BF16),
            jax.ShapeDtypeStruct((d, length, LANES), F32),
        ],
        compiler_params=_params("parallel", "parallel"),
        name=f"window_attn_d{dilation}",
    )(qkv, qkv, qkv, qkv, qkv)


def _split3(a):
    hi = a.astype(BF16)
    r1 = a - hi.astype(F32)
    mid = r1.astype(BF16)
    lo = (r1 - mid.astype(F32)).astype(BF16)
    return hi, mid, lo


def _merge_kernel(x_ref, o1_ref, o4_ref, o16_ref, l1_ref, l4_ref, l16_ref,
                  p4t_ref, p16t_ref, wout_ref, out_ref):
    t, n = o1_ref.shape
    pr = PERM_ROWS
    subs = range(t // pr)
    heads = range(ATTN_HEADS)

    def regrouped_rows(ref, dil, sub):
        per = pr // dil
        return ref[:, sub * per:(sub + 1) * per, :].reshape(pr, ref.shape[2])

    outs, lses = [], []
    for sub in subs:
        rows = slice(sub * pr, (sub + 1) * pr)
        o_sub, l_sub = [o1_ref[rows, :].astype(F32)], [l1_ref[rows, :]]
        for o_ref, l_ref, pt_ref, dil in ((o4_ref, l4_ref, p4t_ref, ATTN_PATTERNS[1][1]),
                                          (o16_ref, l16_ref, p16t_ref, ATTN_PATTERNS[2][1])):
            pt = pt_ref[...]
            o_sub.append(_dot(pt, regrouped_rows(o_ref, dil, sub)))
            hi, mid, lo = _split3(regrouped_rows(l_ref, dil, sub))
            l_sub.append((_dot(pt, hi) + _dot(pt, mid)) + _dot(pt, lo))
        outs.append(o_sub)
        lses.append(l_sub)
    ws = []
    for l_sub in lses:
        m = jnp.maximum(jnp.maximum(l_sub[0], l_sub[1]), l_sub[2])
        es = [jnp.exp(l - m) for l in l_sub]
        inv = 1.0 / (es[0] + es[1] + es[2])
        ws.append([e * inv for e in es])
    merged = []
    for o_sub, w_sub in zip(outs, ws):
        cols = []
        for h in heads:
            hc = slice(h * HEAD_DIM, (h + 1) * HEAD_DIM)
            acc = w_sub[0][:, h:h + 1] * o_sub[0][:, hc]
            acc += w_sub[1][:, h:h + 1] * o_sub[1][:, hc]
            acc += w_sub[2][:, h:h + 1] * o_sub[2][:, hc]
            cols.append(acc)
        merged.append(jnp.concatenate(cols, axis=1).astype(BF16))
    out_ref[...] = x_ref[...] + _dot(jnp.concatenate(merged, axis=0), wout_ref[...])


def _attn_merge(x, outs, lses, perms_t, w_out):
    s, d = x.shape
    n = w_out.shape[0]
    t = MERGE_ROWS
    d4, d16 = ATTN_PATTERNS[1][1], ATTN_PATTERNS[2][1]
    grouped = lambda dil, width: pl.BlockSpec((dil, t // dil, width), lambda i: (0, i, 0))
    return pl.pallas_call(
        _merge_kernel,
        grid=(s // t,),
        in_specs=[
            pl.BlockSpec((t, d), lambda i: (i, 0)),
            pl.BlockSpec((t, n), lambda i: (i, 0)),
            grouped(d4, n),
            grouped(d16, n),
            pl.BlockSpec((t, LANES), lambda i: (i, 0)),
            grouped(d4, LANES),
            grouped(d16, LANES),
            _resident((PERM_ROWS, PERM_ROWS), lambda i: (0, 0)),
            _resident((PERM_ROWS, PERM_ROWS), lambda i: (0, 0)),
            _resident((n, d), lambda i: (0, 0)),
        ],
        out_specs=pl.BlockSpec((t, d), lambda i: (i, 0)),
        out_shape=jax.ShapeDtypeStruct((s, d), F32),
        compiler_params=_params("parallel"),
        name="attn_merge",
    )(x, outs[0], outs[1], outs[2], lses[0], lses[1], lses[2],
      perms_t[0], perms_t[1], w_out)


def _attn_mixer(x, gain, w_qkv, attn_layer, w_out):
    s, d = x.shape
    n = ATTN_HEADS * HEAD_DIM
    perms = [_residue_permutation(PERM_ROWS, dil) for _, dil in ATTN_PATTERNS[1:]]
    p = [jnp.asarray(m, BF16) for m in perms]
    pt = [jnp.asarray(m.T, BF16) for m in perms]
    xn_groups = _norm_perm(x, gain, p)
    outs, lses = [], []
    for g, (_, dil) in enumerate(ATTN_PATTERNS):
        qkv = _qkv_proj(xn_groups[g].reshape(s, d), w_qkv, attn_layer, g)
        o, lse = _attention(qkv.reshape(dil, s // dil, 3 * n), dil)
        if dil == 1:
            o, lse = o.reshape(s, n), lse.reshape(s, LANES)
        outs.append(o)
        lses.append(lse)
    return _attn_merge(x, outs, lses, pt, w_out)


def kernel(x, norm_mix, pool_w_in, pool_w_group, pool_scale, pool_w_out, sgu_w_in, sgu_v_norm, sgu_w_s, sgu_b_s, sgu_w_out, attn_w_qkv, attn_w_out, norm_mlp, mlp_w_up, mlp_w_down, norm_final):
    batch, seq, d = x.shape
    depth = norm_mix.shape[0]
    n_mixers = 3
    bf = lambda w: w.astype(BF16)
    row = lambda v: v.reshape(1, -1)
    outs = []
    seqs = [x.reshape(seq, d)] if batch == 1 else [x[b] for b in range(batch)]
    for h in seqs:
        for i in range(depth):
            kind, j = i % n_mixers, i // n_mixers
            gain = row(norm_mix[i])
            if kind == 0:
                first = (mlp_w_up, mlp_w_down, 0) if i == 0 else None
                h, *first_bf = _pool_mixer(h, gain, bf(pool_w_in[j]), bf(pool_w_group[j]),
                                           row(pool_scale[j]), bf(pool_w_out[j]), first)
                if i == 0:
                    w_up, w_down = first_bf
            elif kind == 1:
                h = _sgu_mixer(h, gain, bf(sgu_w_in[j]), row(sgu_v_norm[j]), bf(sgu_w_s[j]),
                               sgu_b_s[j].T, bf(sgu_w_out[j]))
            else:
                h = _attn_mixer(h, gain, attn_w_qkv, j, bf(attn_w_out[j]))
            nxt = (mlp_w_up, mlp_w_down, i + 1) if i + 1 < depth else None
            h, *next_bf = _mlp(h, row(norm_mlp[i]), w_up, w_down, row(norm_final), nxt)
            if nxt is not None:
                w_up, w_down = next_bf
        outs.append(h)
    return outs[0].reshape(1, seq, d) if batch == 1 else jnp.stack(outs, axis=0)
```

```python
import functools

import jax
import jax.numpy as jnp
import numpy as np
from jax import lax
from jax.experimental import pallas as pl
from jax.experimental.pallas import tpu as pltpu

F32 = jnp.float32
BF16 = jnp.bfloat16

RMS_EPS = 1e-6
POOL_WINDOWS = (2, 4, 8, 16)
SGU_CHUNK = 128
SGU_GROUPS = 8
ATTN_PATTERNS = ((128, 1), (512, 4), (2048, 16))
ATTN_HEADS = 8
HEAD_DIM = 128
ATTN_BLOCK = 128
NEG_INF = -1e30

V7X_VMEM_LIMIT_BYTES = 62 * 1024 * 1024
LANES = 128

MLP_ROWS = 1024
MLP_FF_COLS = 1024
MLP_FIRST_STEP_CHUNKS = 4
PROLOGUE_ROW_CHUNKS = 2
MIX_ROWS = 512
SGU_ROWS = 512
PERM_ROWS = 256
MERGE_ROWS = 512
NORM_PERM_ROWS = 1024
PROJ_ROWS = 2048
ATTN_Q_ROWS = 1024
ATTN_CHAIN_BATCH = 8
POOL_HALO = 32


def _params(*semantics):
    return pltpu.CompilerParams(dimension_semantics=semantics,
                                vmem_limit_bytes=V7X_VMEM_LIMIT_BYTES)


def _resident(shape, index_map):
    return pl.BlockSpec(shape, index_map, pipeline_mode=pl.Buffered(1))


def _rmsnorm(xf, gain_row):
    ms = jnp.mean(xf * xf, axis=-1, keepdims=True)
    return (xf * lax.rsqrt(ms + RMS_EPS)) * gain_row


def _dot(a, b):
    return lax.dot_general(a, b, (((1,), (0,)), ((), ())), preferred_element_type=F32)


def _mlp_weight_cast_specs(w_up, w_down, layer, steps, step_of):
    _, d, ff = w_up.shape
    in_specs = [pl.BlockSpec((None, d // steps, ff), lambda *g: (layer, step_of(*g), 0)),
                pl.BlockSpec((None, ff // steps, d), lambda *g: (layer, step_of(*g), 0))]
    out_specs = [pl.BlockSpec((d // steps, ff), lambda *g: (step_of(*g), 0)),
                 pl.BlockSpec((ff // steps, d), lambda *g: (step_of(*g), 0))]
    out_shape = [jax.ShapeDtypeStruct((d, ff), BF16), jax.ShapeDtypeStruct((ff, d), BF16)]
    return in_specs, out_specs, out_shape


def _cast_mlp_weights(wup_ref, wdn_ref, wup_bf_ref, wdn_bf_ref):
    wup_bf_ref[...] = wup_ref[...].astype(BF16)
    wdn_bf_ref[...] = wdn_ref[...].astype(BF16)


def _mlp_kernel(x_ref, g_ref, wup_ref, wdn_ref, gf_ref, *refs, last):
    if last:
        o_ref, xn_ref = refs
    else:
        nup_ref, ndn_ref, o_ref, nup_bf_ref, ndn_bf_ref, xn_ref = refs
    f = pl.program_id(1)
    chunk = x_ref.shape[0] // MLP_FIRST_STEP_CHUNKS
    row_chunks = [slice(r * chunk, (r + 1) * chunk) for r in range(MLP_FIRST_STEP_CHUNKS)]

    def mlp(xn):
        h = jnp.maximum(_dot(xn, wup_ref[...]), 0.0)
        return _dot((h * h).astype(BF16), wdn_ref[...])

    def cast_next():
        if not last:
            _cast_mlp_weights(nup_ref, ndn_ref, nup_bf_ref, ndn_bf_ref)

    @pl.when(f == 0)
    def _():
        for rows in row_chunks:
            x = x_ref[rows, :]
            xn = _rmsnorm(x, g_ref[...]).astype(BF16)
            xn_ref[rows, :] = xn
            o_ref[rows, :] = x + mlp(xn)
        cast_next()

    @pl.when(f > 0)
    def _():
        o_ref[...] += mlp(xn_ref[...])
        cast_next()

    if last:
        @pl.when(f == pl.num_programs(1) - 1)
        def _():
            o_ref[...] = _rmsnorm(o_ref[...], gf_ref[...])


def _mlp(x, gain, w_up, w_down, final_gain, next_weights):
    s, d = x.shape
    ff = w_up.shape[1]
    tm, tf = MLP_ROWS, MLP_FF_COLS
    last = next_weights is None
    n_chunks = ff // tf
    in_specs = [
        pl.BlockSpec((tm, d), lambda i, f: (i, 0)),
        pl.BlockSpec((1, d), lambda i, f: (0, 0)),
        pl.BlockSpec((d, tf), lambda i, f: (0, f)),
        pl.BlockSpec((tf, d), lambda i, f: (f, 0)),
        pl.BlockSpec((1, d), lambda i, f: (0, 0)),
    ]
    out_specs = [pl.BlockSpec((tm, d), lambda i, f: (i, 0))]
    out_shape = [jax.ShapeDtypeStruct((s, d), F32)]
    operands = [x, gain, w_up, w_down, final_gain]
    if not last:
        cast_in, cast_out, cast_shape = _mlp_weight_cast_specs(
            *next_weights, (s // tm) * n_chunks, lambda i, f: i * n_chunks + f)
        in_specs += cast_in
        out_specs += cast_out
        out_shape += cast_shape
        operands += list(next_weights[:2])
    return pl.pallas_call(
        functools.partial(_mlp_kernel, last=last),
        grid=(s // tm, n_chunks),
        in_specs=in_specs,
        out_specs=out_specs,
        out_shape=out_shape,
        scratch_shapes=[pltpu.VMEM((tm, d), BF16)],
        compiler_params=_params("parallel", "arbitrary"),
        name="mlp_final" if last else "mlp",
    )(*operands)


def _pool_kernel(x_ref, g_ref, win_ref, wg_ref, sc_ref, wout_ref, *refs, cast):
    if cast:
        wup_ref, wdn_ref, o_ref, wup_bf_ref, wdn_bf_ref, hext_ref, t0_ref, t1_ref = refs
    else:
        o_ref, hext_ref, t0_ref, t1_ref = refs
    b = pl.program_id(0)
    tm, d = x_ref.shape
    halo = POOL_HALO
    gc = d // len(POOL_WINDOWS)

    @pl.when(b == 0)
    def _():
        hext_ref[0:halo, :] = jnp.zeros((halo, d), F32)
        t0_ref[0:halo, :] = jnp.zeros((halo, gc), F32)
        t1_ref[0:halo, :] = jnp.zeros((halo, gc), F32)

    @pl.when(b > 0)
    def _():
        hext_ref[halo - 16:halo, :] = hext_ref[tm + halo - 16:tm + halo, :]

    chunk = tm // PROLOGUE_ROW_CHUNKS
    for r in range(PROLOGUE_ROW_CHUNKS):
        xn = _rmsnorm(x_ref[r * chunk:(r + 1) * chunk, :], g_ref[...]).astype(BF16)
        hext_ref[halo + r * chunk:halo + (r + 1) * chunk, :] = _dot(xn, win_ref[...])
    x = x_ref[...]

    pos = b * tm + lax.broadcasted_iota(jnp.int32, (tm, 1), 0)
    temps = (t0_ref, t1_ref)
    mixed = []
    for g, w in enumerate(POOL_WINDOWS):
        cols = slice(g * gc, (g + 1) * gc)
        src, src_cols, shift, levels = hext_ref, cols, 1, g + 1
        for lvl in range(levels):
            last = lvl == levels - 1
            lo = halo if last else halo - 16
            n = tm if last else tm + 16
            s = src[lo:lo + n, src_cols] + src[lo - shift:lo - shift + n, src_cols]
            if not last:
                dst = temps[lvl % 2]
                dst[lo:lo + n, :] = s
                src, src_cols = dst, slice(None)
            shift *= 2
        inv_count = 1.0 / jnp.minimum(pos + 1, w).astype(F32)
        pooled = s * inv_count - hext_ref[halo:halo + tm, cols]
        mixed.append(_dot(pooled.astype(BF16), wg_ref[g]))
    mixed = (jnp.concatenate(mixed, axis=1) * sc_ref[...]).astype(BF16)
    o_ref[...] = x + _dot(mixed, wout_ref[...])
    if cast:
        _cast_mlp_weights(wup_ref, wdn_ref, wup_bf_ref, wdn_bf_ref)


def _pool_mixer(x, gain, w_in, w_group, scale, w_out, cast_weights):
    s, d = x.shape
    tm = MIX_ROWS
    ng, gc, _ = w_group.shape
    steps = s // tm
    in_specs = [
        pl.BlockSpec((tm, d), lambda i: (i, 0)),
        _resident((1, d), lambda i: (0, 0)),
        _resident((d, d), lambda i: (0, 0)),
        _resident((ng, gc, gc), lambda i: (0, 0, 0)),
        _resident((1, d), lambda i: (0, 0)),
        _resident((d, d), lambda i: (0, 0)),
    ]
    out_specs = [pl.BlockSpec((tm, d), lambda i: (i, 0))]
    out_shape = [jax.ShapeDtypeStruct((s, d), F32)]
    operands = [x, gain, w_in, w_group, scale, w_out]
    cast = cast_weights is not None
    if cast:
        cast_in, cast_out, cast_shape = _mlp_weight_cast_specs(*cast_weights, steps, lambda i: i)
        in_specs += cast_in
        out_specs += cast_out
        out_shape += cast_shape
        operands += list(cast_weights[:2])
    return pl.pallas_call(
        functools.partial(_pool_kernel, cast=cast),
        grid=(steps,),
        in_specs=in_specs,
        out_specs=out_specs,
        out_shape=out_shape,
        scratch_shapes=[pltpu.VMEM((tm + POOL_HALO, d), F32),
                        pltpu.VMEM((tm + POOL_HALO, gc), F32),
                        pltpu.VMEM((tm + POOL_HALO, gc), F32)],
        compiler_params=_params("arbitrary"),
        name="pool_mixer",
    )(*operands)


def _gelu(x):
    return 0.5 * x * (1.0 + lax.erf(x * np.float32(np.sqrt(0.5))))


def _sgu_kernel(x_ref, g_ref, win_ref, vn_ref, ws_ref, bs_ref, wout_ref, o_ref,
                u_ref, v_ref, gated_ref):
    tm, d = x_ref.shape
    e = win_ref.shape[1] // 2
    gcols = e // SGU_GROUPS
    chunk = tm // PROLOGUE_ROW_CHUNKS
    for r in range(PROLOGUE_ROW_CHUNKS):
        rows = slice(r * chunk, (r + 1) * chunk)
        xn = _rmsnorm(x_ref[rows, :], g_ref[...]).astype(BF16)
        u_ref[rows, :] = _gelu(_dot(xn, win_ref[:, :e]))
        v = _gelu(_dot(xn, win_ref[:, e:]))
        v_ref[rows, :] = _rmsnorm(v, vn_ref[...]).astype(BF16)
    x = x_ref[...]

    t = lax.broadcasted_iota(jnp.int32, (SGU_CHUNK, SGU_CHUNK), 0)
    sidx = lax.broadcasted_iota(jnp.int32, (SGU_CHUNK, SGU_CHUNK), 1)
    causal = sidx <= t
    for g in range(SGU_GROUPS):
        ws = jnp.where(causal, ws_ref[g], jnp.zeros_like(ws_ref[g]))
        bias = bs_ref[:, g:g + 1]
        cols = slice(g * gcols, (g + 1) * gcols)
        for c in range(tm // SGU_CHUNK):
            rows = slice(c * SGU_CHUNK, (c + 1) * SGU_CHUNK)
            sp = _dot(ws, v_ref[rows, cols]) + bias
            gated_ref[rows, cols] = (u_ref[rows, cols] * sp).astype(BF16)
    o_ref[...] = x + _dot(gated_ref[...], wout_ref[...])


def _sgu_mixer(x, gain, w_in, v_norm, w_s, b_s_t, w_out):
    s, d = x.shape
    tm = SGU_ROWS
    e = w_out.shape[0]
    return pl.pallas_call(
        _sgu_kernel,
        grid=(s // tm,),
        in_specs=[
            pl.BlockSpec((tm, d), lambda i: (i, 0)),
            _resident((1, d), lambda i: (0, 0)),
            _resident((d, 2 * e), lambda i: (0, 0)),
            _resident((1, e), lambda i: (0, 0)),
            _resident(w_s.shape, lambda i: (0, 0, 0)),
            _resident(b_s_t.shape, lambda i: (0, 0)),
            _resident((e, d), lambda i: (0, 0)),
        ],
        out_specs=pl.BlockSpec((tm, d), lambda i: (i, 0)),
        out_shape=jax.ShapeDtypeStruct((s, d), F32),
        scratch_shapes=[pltpu.VMEM((tm, e), F32),
                        pltpu.VMEM((tm, e), BF16),
                        pltpu.VMEM((tm, e), BF16)],
        compiler_params=_params("parallel"),
        name="sgu_mixer",
    )(x, gain, w_in, v_norm, w_s, b_s_t, w_out)


def _residue_permutation(rows, dilation):
    n = np.arange(rows)
    per = rows // dilation
    old = dilation * (n % per) + n // per
    p = np.zeros((rows, rows), np.float32)
    p[n, old] = 1.0
    return p


def _norm_perm_kernel(x_ref, g_ref, p4_ref, p16_ref, o1_ref, o4_ref, o16_ref):
    t = x_ref.shape[0]
    pr = PERM_ROWS
    xn = _rmsnorm(x_ref[...], g_ref[...]).astype(BF16)
    o1_ref[...] = xn
    for p_ref, o_ref in ((p4_ref, o4_ref), (p16_ref, o16_ref)):
        dil = o_ref.shape[0]
        per = pr // dil
        for sub in range(t // pr):
            y = _dot(p_ref[...], xn[sub * pr:(sub + 1) * pr, :]).astype(BF16)
            o_ref[:, sub * per:(sub + 1) * per, :] = y.reshape(dil, per, y.shape[1])


def _norm_perm(x, gain, perms):
    s, d = x.shape
    t = NORM_PERM_ROWS
    d4, d16 = ATTN_PATTERNS[1][1], ATTN_PATTERNS[2][1]
    return pl.pallas_call(
        _norm_perm_kernel,
        grid=(s // t,),
        in_specs=[
            pl.BlockSpec((t, d), lambda i: (i, 0)),
            _resident((1, d), lambda i: (0, 0)),
            _resident((PERM_ROWS, PERM_ROWS), lambda i: (0, 0)),
            _resident((PERM_ROWS, PERM_ROWS), lambda i: (0, 0)),
        ],
        out_specs=[
            pl.BlockSpec((t, d), lambda i: (i, 0)),
            pl.BlockSpec((d4, t // d4, d), lambda i: (0, i, 0)),
            pl.BlockSpec((d16, t // d16, d), lambda i: (0, i, 0)),
        ],
        out_shape=[
            jax.ShapeDtypeStruct((s, d), BF16),
            jax.ShapeDtypeStruct((d4, s // d4, d), BF16),
            jax.ShapeDtypeStruct((d16, s // d16, d), BF16),
        ],
        compiler_params=_params("parallel"),
        name="attn_norm_perm",
    )(x, gain, perms[0], perms[1])


def _proj_kernel(a_ref, w_ref, o_ref):
    o_ref[...] = _dot(a_ref[...], w_ref[...]).astype(o_ref.dtype)


def _qkv_proj(a, w_qkv, layer, group):
    s, d = a.shape
    n = ATTN_HEADS * HEAD_DIM
    n_groups = len(ATTN_PATTERNS)
    tm = PROJ_ROWS
    return pl.pallas_call(
        _proj_kernel,
        grid=(3, s // tm),
        in_specs=[
            pl.BlockSpec((tm, d), lambda j, i: (i, 0)),
            pl.BlockSpec((None, d, n), lambda j, i: (layer, 0, j * n_groups + group)),
        ],
        out_specs=pl.BlockSpec((tm, n), lambda j, i: (i, j)),
        out_shape=jax.ShapeDtypeStruct((s, 3 * n), BF16),
        compiler_params=_params("parallel", "parallel"),
        name=f"qkv_proj_g{group}",
    )(a, w_qkv)


def _attn_kernel(q_ref, kp_ref, kc_ref, vp_ref, vc_ref, o_ref, lse_ref):
    ib = pl.program_id(1)
    tq = q_ref.shape[1]
    blk = ATTN_BLOCK
    scale = HEAD_DIM ** -0.5
    heads = range(ATTN_HEADS)
    row = lax.broadcasted_iota(jnp.int32, (blk, 2 * blk), 0)
    col = lax.broadcasted_iota(jnp.int32, (blk, 2 * blk), 1)
    band = jnp.logical_and(col >= row, col <= row + blk)
    first_band = jnp.logical_and(band, jnp.logical_or(col >= blk, ib > 0))
    lane = lax.broadcasted_iota(jnp.int32, (blk, LANES), 1)
    contract_last = (((1,), (1,)), ((), ()))

    def window(prev_ref, cur_ref, sub, hc):
        if sub == 0:
            return jnp.concatenate([prev_ref[0, :, hc], cur_ref[0, 0:blk, hc]], axis=0)
        return cur_ref[0, (sub - 1) * blk:(sub + 1) * blk, hc]

    chains = [(sub, h) for sub in range(tq // blk) for h in range(ATTN_HEADS)]
    lse_rows = {}
    for c0 in range(0, len(chains), ATTN_CHAIN_BATCH):
        batch = chains[c0:c0 + ATTN_CHAIN_BATCH]
        rows = [slice(sub * blk, (sub + 1) * blk) for sub, _ in batch]
        hcs = [slice(h * HEAD_DIM, (h + 1) * HEAD_DIM) for _, h in batch]
        s = [lax.dot_general(q_ref[0, r, hc], window(kp_ref, kc_ref, sub, hc), contract_last,
                             preferred_element_type=F32)
             for (sub, _), r, hc in zip(batch, rows, hcs)]
        s = [jnp.where(first_band if sub == 0 else band, sh * scale, NEG_INF)
             for (sub, _), sh in zip(batch, s)]
        m = [jnp.max(sh, axis=-1, keepdims=True) for sh in s]
        p = [jnp.exp(sh - mh) for sh, mh in zip(s, m)]
        den = [jnp.sum(ph, axis=-1, keepdims=True) for ph in p]
        o = [_dot(ph.astype(BF16), window(vp_ref, vc_ref, sub, hc))
             for (sub, _), ph, hc in zip(batch, p, hcs)]
        for k, (sub, h) in enumerate(batch):
            o_ref[0, rows[k], hcs[k]] = (o[k] / den[k]).astype(o_ref.dtype)
            acc = lse_rows.get(sub, jnp.zeros((blk, LANES), F32))
            lse_rows[sub] = jnp.where(lane == h, m[k] + jnp.log(den[k]), acc)
    for sub, v in lse_rows.items():
        lse_ref[0, sub * blk:(sub + 1) * blk, :] = v


def _attention(qkv, dilation):
    d, length, n3 = qkv.shape
    n = n3 // 3
    tq = ATTN_Q_ROWS
    sub_per_tile = tq // ATTN_BLOCK
    prev_map = lambda col: (lambda r, i: (r, jnp.maximum(i * sub_per_tile - 1, 0), col))
    cur_map = lambda col: (lambda r, i: (r, i, col))
    return pl.pallas_call(
        _attn_kernel,
        grid=(d, length // tq),
        in_specs=[
            pl.BlockSpec((1, tq, n), cur_map(0)),
            pl.BlockSpec((1, ATTN_BLOCK, n), prev_map(1)),
            pl.BlockSpec((1, tq, n), cur_map(1)),
            pl.BlockSpec((1, ATTN_BLOCK, n), prev_map(2)),
            pl.BlockSpec((1, tq, n), cur_map(2)),
        ],
        out_specs=[
            pl.BlockSpec((1, tq, n), cur_map(0)),
            pl.BlockSpec((1, tq, LANES), cur_map(0)),
        ],
        out_shape=[
            jax.ShapeDtypeStruct((d, length, n), BF16),
            jax.ShapeDtypeStruct((d, length, LANES), F32),
        ],
        compiler_params=_params("parallel", "parallel"),
        name=f"window_attn_d{dilation}",
    )(qkv, qkv, qkv, qkv, qkv)


def _split3(a):
    hi = a.astype(BF16)
    r1 = a - hi.astype(F32)
    mid = r1.astype(BF16)
    lo = (r1 - mid.astype(F32)).astype(BF16)
    return hi, mid, lo


def _merge_kernel(x_ref, o1_ref, o4_ref, o16_ref, l1_ref, l4_ref, l16_ref,
                  p4t_ref, p16t_ref, wout_ref, out_ref):
    t, n = o1_ref.shape
    pr = PERM_ROWS
    subs = range(t // pr)
    heads = range(ATTN_HEADS)

    def regrouped_rows(ref, dil, sub):
        per = pr // dil
        return ref[:, sub * per:(sub + 1) * per, :].reshape(pr, ref.shape[2])

    outs, lses = [], []
    for sub in subs:
        rows = slice(sub * pr, (sub + 1) * pr)
        o_sub, l_sub = [o1_ref[rows, :].astype(F32)], [l1_ref[rows, :]]
        for o_ref, l_ref, pt_ref, dil in ((o4_ref, l4_ref, p4t_ref, ATTN_PATTERNS[1][1]),
                                          (o16_ref, l16_ref, p16t_ref, ATTN_PATTERNS[2][1])):
            pt = pt_ref[...]
            o_sub.append(_dot(pt, regrouped_rows(o_ref, dil, sub)))
            hi, mid, lo = _split3(regrouped_rows(l_ref, dil, sub))
            l_sub.append((_dot(pt, hi) + _dot(pt, mid)) + _dot(pt, lo))
        outs.append(o_sub)
        lses.append(l_sub)
    ws = []
    for l_sub in lses:
        m = jnp.maximum(jnp.maximum(l_sub[0], l_sub[1]), l_sub[2])
        es = [jnp.exp(l - m) for l in l_sub]
        inv = 1.0 / (es[0] + es[1] + es[2])
        ws.append([e * inv for e in es])
    merged = []
    for o_sub, w_sub in zip(outs, ws):
        cols = []
        for h in heads:
            hc = slice(h * HEAD_DIM, (h + 1) * HEAD_DIM)
            acc = w_sub[0][:, h:h + 1] * o_sub[0][:, hc]
            acc += w_sub[1][:, h:h + 1] * o_sub[1][:, hc]
            acc += w_sub[2][:, h:h + 1] * o_sub[2][:, hc]
            cols.append(acc)
        merged.append(jnp.concatenate(cols, axis=1).astype(BF16))
    out_ref[...] = x_ref[...] + _dot(jnp.concatenate(merged, axis=0), wout_ref[...])


def _attn_merge(x, outs, lses, perms_t, w_out):
    s, d = x.shape
    n = w_out.shape[0]
    t = MERGE_ROWS
    d4, d16 = ATTN_PATTERNS[1][1], ATTN_PATTERNS[2][1]
    grouped = lambda dil, width: pl.BlockSpec((dil, t // dil, width), lambda i: (0, i, 0))
    return pl.pallas_call(
        _merge_kernel,
        grid=(s // t,),
        in_specs=[
            pl.BlockSpec((t, d), lambda i: (i, 0)),
            pl.BlockSpec((t, n), lambda i: (i, 0)),
            grouped(d4, n),
            grouped(d16, n),
            pl.BlockSpec((t, LANES), lambda i: (i, 0)),
            grouped(d4, LANES),
            grouped(d16, LANES),
            _resident((PERM_ROWS, PERM_ROWS), lambda i: (0, 0)),
            _resident((PERM_ROWS, PERM_ROWS), lambda i: (0, 0)),
            _resident((n, d), lambda i: (0, 0)),
        ],
        out_specs=pl.BlockSpec((t, d), lambda i: (i, 0)),
        out_shape=jax.ShapeDtypeStruct((s, d), F32),
        compiler_params=_params("parallel"),
        name="attn_merge",
    )(x, outs[0], outs[1], outs[2], lses[0], lses[1], lses[2],
      perms_t[0], perms_t[1], w_out)


def _attn_mixer(x, gain, w_qkv, attn_layer, w_out):
    s, d = x.shape
    n = ATTN_HEADS * HEAD_DIM
    perms = [_residue_permutation(PERM_ROWS, dil) for _, dil in ATTN_PATTERNS[1:]]
    p = [jnp.asarray(m, BF16) for m in perms]
    pt = [jnp.asarray(m.T, BF16) for m in perms]
    xn_groups = _norm_perm(x, gain, p)
    outs, lses = [], []
    for g, (_, dil) in enumerate(ATTN_PATTERNS):
        qkv = _qkv_proj(xn_groups[g].reshape(s, d), w_qkv, attn_layer, g)
        o, lse = _attention(qkv.reshape(dil, s // dil, 3 * n), dil)
        if dil == 1:
            o, lse = o.reshape(s, n), lse.reshape(s, LANES)
        outs.append(o)
        lses.append(lse)
    return _attn_merge(x, outs, lses, pt, w_out)


def kernel(x, norm_mix, pool_w_in, pool_w_group, pool_scale, pool_w_out, sgu_w_in, sgu_v_norm, sgu_w_s, sgu_b_s, sgu_w_out, attn_w_qkv, attn_w_out, norm_mlp, mlp_w_up, mlp_w_down, norm_final):
    batch, seq, d = x.shape
    depth = norm_mix.shape[0]
    n_mixers = 3
    bf = lambda w: w.astype(BF16)
    row = lambda v: v.reshape(1, -1)
    outs = []
    seqs = [x.reshape(seq, d)] if batch == 1 else [x[b] for b in range(batch)]
    for h in seqs:
        for i in range(depth):
            kind, j = i % n_mixers, i // n_mixers
            gain = row(norm_mix[i])
            if kind == 0:
                first = (mlp_w_up, mlp_w_down, 0) if i == 0 else None
                h, *first_bf = _pool_mixer(h, gain, bf(pool_w_in[j]), bf(pool_w_group[j]),
                                           row(pool_scale[j]), bf(pool_w_out[j]), first)
                if i == 0:
                    w_up, w_down = first_bf
            elif kind == 1:
                h = _sgu_mixer(h, gain, bf(sgu_w_in[j]), row(sgu_v_norm[j]), bf(sgu_w_s[j]),
                               sgu_b_s[j].T, bf(sgu_w_out[j]))
            else:
                h = _attn_mixer(h, gain, attn_w_qkv, j, bf(attn_w_out[j]))
            nxt = (mlp_w_up, mlp_w_down, i + 1) if i + 1 < depth else None
            h, *next_bf = _mlp(h, row(norm_mlp[i]), w_up, w_down, row(norm_final), nxt)
            if nxt is not None:
                w_up, w_down = next_bf
        outs.append(h)
    return outs[0].reshape(1, seq, d) if batch == 1 else jnp.stack(outs, axis=0)
```

```python
import functools

import jax
import jax.numpy as jnp
import numpy as np
from jax import lax
from jax.experimental import pallas as pl
from jax.experimental.pallas import tpu as pltpu

F32 = jnp.float32
BF16 = jnp.bfloat16

RMS_EPS = 1e-6
POOL_WINDOWS = (2, 4, 8, 16)
SGU_CHUNK = 128
SGU_GROUPS = 8
ATTN_PATTERNS = ((128, 1), (512, 4), (2048, 16))
ATTN_HEADS = 8
HEAD_DIM = 128
ATTN_BLOCK = 128
NEG_INF = -1e30

V7X_VMEM_LIMIT_BYTES = 62 * 1024 * 1024
LANES = 128

MLP_ROWS = 1024
MLP_FF_COLS = 1024
MLP_FIRST_STEP_CHUNKS = 4
PROLOGUE_ROW_CHUNKS = 2
MIX_ROWS = 512
SGU_ROWS = 512
PERM_ROWS = 256
MERGE_ROWS = 512
NORM_PERM_ROWS = 1024
PROJ_ROWS = 2048
ATTN_Q_ROWS = 1024
ATTN_CHAIN_BATCH = 8
POOL_HALO = 32


def _params(*semantics):
    return pltpu.CompilerParams(dimension_semantics=semantics,
                                vmem_limit_bytes=V7X_VMEM_LIMIT_BYTES)


def _resident(shape, index_map):
    return pl.BlockSpec(shape, index_map, pipeline_mode=pl.Buffered(1))


def _rmsnorm(xf, gain_row):
    ms = jnp.mean(xf * xf, axis=-1, keepdims=True)
    return (xf * lax.rsqrt(ms + RMS_EPS)) * gain_row


def _dot(a, b):
    return lax.dot_general(a, b, (((1,), (0,)), ((), ())), preferred_element_type=F32)


def _mlp_weight_cast_specs(w_up, w_down, layer, steps, step_of):
    _, d, ff = w_up.shape
    in_specs = [pl.BlockSpec((None, d // steps, ff), lambda *g: (layer, step_of(*g), 0)),
                pl.BlockSpec((None, ff // steps, d), lambda *g: (layer, step_of(*g), 0))]
    out_specs = [pl.BlockSpec((d // steps, ff), lambda *g: (step_of(*g), 0)),
                 pl.BlockSpec((ff // steps, d), lambda *g: (step_of(*g), 0))]
    out_shape = [jax.ShapeDtypeStruct((d, ff), BF16), jax.ShapeDtypeStruct((ff, d), BF16)]
    return in_specs, out_specs, out_shape


def _cast_mlp_weights(wup_ref, wdn_ref, wup_bf_ref, wdn_bf_ref):
    wup_bf_ref[...] = wup_ref[...].astype(BF16)
    wdn_bf_ref[...] = wdn_ref[...].astype(BF16)


def _mlp_kernel(x_ref, g_ref, wup_ref, wdn_ref, gf_ref, *refs, last):
    if last:
        o_ref, xn_ref = refs
    else:
        nup_ref, ndn_ref, o_ref, nup_bf_ref, ndn_bf_ref, xn_ref = refs
    f = pl.program_id(1)
    chunk = x_ref.shape[0] // MLP_FIRST_STEP_CHUNKS
    row_chunks = [slice(r * chunk, (r + 1) * chunk) for r in range(MLP_FIRST_STEP_CHUNKS)]

    def mlp(xn):
        h = jnp.maximum(_dot(xn, wup_ref[...]), 0.0)
        return _dot((h * h).astype(BF16), wdn_ref[...])

    def cast_next():
        if not last:
            _cast_mlp_weights(nup_ref, ndn_ref, nup_bf_ref, ndn_bf_ref)

    @pl.when(f == 0)
    def _():
        for rows in row_chunks:
            x = x_ref[rows, :]
            xn = _rmsnorm(x, g_ref[...]).astype(BF16)
            xn_ref[rows, :] = xn
            o_ref[rows, :] = x + mlp(xn)
        cast_next()

    @pl.when(f > 0)
    def _():
        o_ref[...] += mlp(xn_ref[...])
        cast_next()

    if last:
        @pl.when(f == pl.num_programs(1) - 1)
        def _():
            o_ref[...] = _rmsnorm(o_ref[...], gf_ref[...])


def _mlp(x, gain, w_up, w_down, final_gain, next_weights):
    s, d = x.shape
    ff = w_up.shape[1]
    tm, tf = MLP_ROWS, MLP_FF_COLS
    last = next_weights is None
    n_chunks = ff // tf
    in_specs = [
        pl.BlockSpec((tm, d), lambda i, f: (i, 0)),
        pl.BlockSpec((1, d), lambda i, f: (0, 0)),
        pl.BlockSpec((d, tf), lambda i, f: (0, f)),
        pl.BlockSpec((tf, d), lambda i, f: (f, 0)),
        pl.BlockSpec((1, d), lambda i, f: (0, 0)),
    ]
    out_specs = [pl.BlockSpec((tm, d), lambda i, f: (i, 0))]
    out_shape = [jax.ShapeDtypeStruct((s, d), F32)]
    operands = [x, gain, w_up, w_down, final_gain]
    if not last:
        cast_in, cast_out, cast_shape = _mlp_weight_cast_specs(
            *next_weights, (s // tm) * n_chunks, lambda i, f: i * n_chunks + f)
        in_specs += cast_in
        out_specs += cast_out
        out_shape += cast_shape
        operands += list(next_weights[:2])
    return pl.pallas_call(
        functools.partial(_mlp_kernel, last=last),
        grid=(s // tm, n_chunks),
        in_specs=in_specs,
        out_specs=out_specs,
        out_shape=out_shape,
        scratch_shapes=[pltpu.VMEM((tm, d), BF16)],
        compiler_params=_params("parallel", "arbitrary"),
        name="mlp_final" if last else "mlp",
    )(*operands)


def _pool_kernel(x_ref, g_ref, win_ref, wg_ref, sc_ref, wout_ref, *refs, cast):
    if cast:
        wup_ref, wdn_ref, o_ref, wup_bf_ref, wdn_bf_ref, hext_ref, t0_ref, t1_ref = refs
    else:
        o_ref, hext_ref, t0_ref, t1_ref = refs
    b = pl.program_id(0)
    tm, d = x_ref.shape
    halo = POOL_HALO
    gc = d // len(POOL_WINDOWS)

    @pl.when(b == 0)
    def _():
        hext_ref[0:halo, :] = jnp.zeros((halo, d), F32)
        t0_ref[0:halo, :] = jnp.zeros((halo, gc), F32)
        t1_ref[0:halo, :] = jnp.zeros((halo, gc), F32)

    @pl.when(b > 0)
    def _():
        hext_ref[halo - 16:halo, :] = hext_ref[tm + halo - 16:tm + halo, :]

    chunk = tm // PROLOGUE_ROW_CHUNKS
    for r in range(PROLOGUE_ROW_CHUNKS):
        xn = _rmsnorm(x_ref[r * chunk:(r + 1) * chunk, :], g_ref[...]).astype(BF16)
        hext_ref[halo + r * chunk:halo + (r + 1) * chunk, :] = _dot(xn, win_ref[...])
    x = x_ref[...]

    pos = b * tm + lax.broadcasted_iota(jnp.int32, (tm, 1), 0)
    temps = (t0_ref, t1_ref)
    mixed = []
    for g, w in enumerate(POOL_WINDOWS):
        cols = slice(g * gc, (g + 1) * gc)
        src, src_cols, shift, levels = hext_ref, cols, 1, g + 1
        for lvl in range(levels):
            last = lvl == levels - 1
            lo = halo if last else halo - 16
            n = tm if last else tm + 16
            s = src[lo:lo + n, src_cols] + src[lo - shift:lo - shift + n, src_cols]
            if not last:
                dst = temps[lvl % 2]
                dst[lo:lo + n, :] = s
                src, src_cols = dst, slice(None)
            shift *= 2
        inv_count = 1.0 / jnp.minimum(pos + 1, w).astype(F32)
        pooled = s * inv_count - hext_ref[halo:halo + tm, cols]
        mixed.append(_dot(pooled.astype(BF16), wg_ref[g]))
    mixed = (jnp.concatenate(mixed, axis=1) * sc_ref[...]).astype(BF16)
    o_ref[...] = x + _dot(mixed, wout_ref[...])
    if cast:
        _cast_mlp_weights(wup_ref, wdn_ref, wup_bf_ref, wdn_bf_ref)


def _pool_mixer(x, gain, w_in, w_group, scale, w_out, cast_weights):
    s, d = x.shape
    tm = MIX_ROWS
    ng, gc, _ = w_group.shape
    steps = s // tm
    in_specs = [
        pl.BlockSpec((tm, d), lambda i: (i, 0)),
        _resident((1, d), lambda i: (0, 0)),
        _resident((d, d), lambda i: (0, 0)),
        _resident((ng, gc, gc), lambda i: (0, 0, 0)),
        _resident((1, d), lambda i: (0, 0)),
        _resident((d, d), lambda i: (0, 0)),
    ]
    out_specs = [pl.BlockSpec((tm, d), lambda i: (i, 0))]
    out_shape = [jax.ShapeDtypeStruct((s, d), F32)]
    operands = [x, gain, w_in, w_group, scale, w_out]
    cast = cast_weights is not None
    if cast:
        cast_in, cast_out, cast_shape = _mlp_weight_cast_specs(*cast_weights, steps, lambda i: i)
        in_specs += cast_in
        out_specs += cast_out
        out_shape += cast_shape
        operands += list(cast_weights[:2])
    return pl.pallas_call(
        functools.partial(_pool_kernel, cast=cast),
        grid=(steps,),
        in_specs=in_specs,
        out_specs=out_specs,
        out_shape=out_shape,
        scratch_shapes=[pltpu.VMEM((tm + POOL_HALO, d), F32),
                        pltpu.VMEM((tm + POOL_HALO, gc), F32),
                        pltpu.VMEM((tm + POOL_HALO, gc), F32)],
        compiler_params=_params("arbitrary"),
        name="pool_mixer",
    )(*operands)


def _gelu(x):
    return 0.5 * x * (1.0 + lax.erf(x * np.float32(np.sqrt(0.5))))


def _sgu_kernel(x_ref, g_ref, win_ref, vn_ref, ws_ref, bs_ref, wout_ref, o_ref,
                u_ref, v_ref, gated_ref):
    tm, d = x_ref.shape
    e = win_ref.shape[1] // 2
    gcols = e // SGU_GROUPS
    chunk = tm // PROLOGUE_ROW_CHUNKS
    for r in range(PROLOGUE_ROW_CHUNKS):
        rows = slice(r * chunk, (r + 1) * chunk)
        xn = _rmsnorm(x_ref[rows, :], g_ref[...]).astype(BF16)
        u_ref[rows, :] = _gelu(_dot(xn, win_ref[:, :e]))
        v = _gelu(_dot(xn, win_ref[:, e:]))
        v_ref[rows, :] = _rmsnorm(v, vn_ref[...]).astype(BF16)
    x = x_ref[...]

    t = lax.broadcasted_iota(jnp.int32, (SGU_CHUNK, SGU_CHUNK), 0)
    sidx = lax.broadcasted_iota(jnp.int32, (SGU_CHUNK, SGU_CHUNK), 1)
    causal = sidx <= t
    for g in range(SGU_GROUPS):
        ws = jnp.where(causal, ws_ref[g], jnp.zeros_like(ws_ref[g]))
        bias = bs_ref[:, g:g + 1]
        cols = slice(g * gcols, (g + 1) * gcols)
        for c in range(tm // SGU_CHUNK):
            rows = slice(c * SGU_CHUNK, (c + 1) * SGU_CHUNK)
            sp = _dot(ws, v_ref[rows, cols]) + bias
            gated_ref[rows, cols] = (u_ref[rows, cols] * sp).astype(BF16)
    o_ref[...] = x + _dot(gated_ref[...], wout_ref[...])


def _sgu_mixer(x, gain, w_in, v_norm, w_s, b_s_t, w_out):
    s, d = x.shape
    tm = SGU_ROWS
    e = w_out.shape[0]
    return pl.pallas_call(
        _sgu_kernel,
        grid=(s // tm,),
        in_specs=[
            pl.BlockSpec((tm, d), lambda i: (i, 0)),
            _resident((1, d), lambda i: (0, 0)),
            _resident((d, 2 * e), lambda i: (0, 0)),
            _resident((1, e), lambda i: (0, 0)),
            _resident(w_s.shape, lambda i: (0, 0, 0)),
            _resident(b_s_t.shape, lambda i: (0, 0)),
            _resident((e, d), lambda i: (0, 0)),
        ],
        out_specs=pl.BlockSpec((tm, d), lambda i: (i, 0)),
        out_shape=jax.ShapeDtypeStruct((s, d), F32),
        scratch_shapes=[pltpu.VMEM((tm, e), F32),
                        pltpu.VMEM((tm, e), BF16),
                        pltpu.VMEM((tm, e), BF16)],
        compiler_params=_params("parallel"),
        name="sgu_mixer",
    )(x, gain, w_in, v_norm, w_s, b_s_t, w_out)


def _residue_permutation(rows, dilation):
    n = np.arange(rows)
    per = rows // dilation
    old = dilation * (n % per) + n // per
    p = np.zeros((rows, rows), np.float32)
    p[n, old] = 1.0
    return p


def _norm_perm_kernel(x_ref, g_ref, p4_ref, p16_ref, o1_ref, o4_ref, o16_ref):
    t = x_ref.shape[0]
    pr = PERM_ROWS
    xn = _rmsnorm(x_ref[...], g_ref[...]).astype(BF16)
    o1_ref[...] = xn
    for p_ref, o_ref in ((p4_ref, o4_ref), (p16_ref, o16_ref)):
        dil = o_ref.shape[0]
        per = pr // dil
        for sub in range(t // pr):
            y = _dot(p_ref[...], xn[sub * pr:(sub + 1) * pr, :]).astype(BF16)
            o_ref[:, sub * per:(sub + 1) * per, :] = y.reshape(dil, per, y.shape[1])


def _norm_perm(x, gain, perms):
    s, d = x.shape
    t = NORM_PERM_ROWS
    d4, d16 = ATTN_PATTERNS[1][1], ATTN_PATTERNS[2][1]
    return pl.pallas_call(
        _norm_perm_kernel,
        grid=(s // t,),
        in_specs=[
            pl.BlockSpec((t, d), lambda i: (i, 0)),
            _resident((1, d), lambda i: (0, 0)),
            _resident((PERM_ROWS, PERM_ROWS), lambda i: (0, 0)),
            _resident((PERM_ROWS, PERM_ROWS), lambda i: (0, 0)),
        ],
        out_specs=[
            pl.BlockSpec((t, d), lambda i: (i, 0)),
            pl.BlockSpec((d4, t // d4, d), lambda i: (0, i, 0)),
            pl.BlockSpec((d16, t // d16, d), lambda i: (0, i, 0)),
        ],
        out_shape=[
            jax.ShapeDtypeStruct((s, d), BF16),
            jax.ShapeDtypeStruct((d4, s // d4, d), BF16),
            jax.ShapeDtypeStruct((d16, s // d16, d), BF16),
        ],
        compiler_params=_params("parallel"),
        name="attn_norm_perm",
    )(x, gain, perms[0], perms[1])


def _proj_kernel(a_ref, w_ref, *refs, cast):
    if cast:
        r_ref, o_ref, r_bf_ref = refs
        r_bf_ref[...] = r_ref[...].astype(BF16)
    else:
        (o_ref,) = refs
    o_ref[...] = _dot(a_ref[...], w_ref[...]).astype(o_ref.dtype)


def _qkv_proj(a, w_qkv, layer, group, hosted):
    s, d = a.shape
    n = ATTN_HEADS * HEAD_DIM
    n_groups = len(ATTN_PATTERNS)
    tm = PROJ_ROWS
    blocks = s // tm
    in_specs = [
        pl.BlockSpec((tm, d), lambda j, i: (i, 0)),
        pl.BlockSpec((None, d, n), lambda j, i: (layer, 0, j * n_groups + group)),
    ]
    out_specs = [pl.BlockSpec((tm, n), lambda j, i: (i, j))]
    out_shape = [jax.ShapeDtypeStruct((s, 3 * n), BF16)]
    operands = [a, w_qkv]
    if hosted is not None:
        stack, hosted_layer = hosted
        _, r, c = stack.shape
        in_specs.append(pl.BlockSpec((None, r // blocks, c), lambda j, i: (hosted_layer, i, 0)))
        out_specs.append(pl.BlockSpec((r // blocks, c), lambda j, i: (i, 0)))
        out_shape.append(jax.ShapeDtypeStruct((r, c), BF16))
        operands.append(stack)
    return pl.pallas_call(
        functools.partial(_proj_kernel, cast=hosted is not None),
        grid=(3, blocks),
        in_specs=in_specs,
        out_specs=out_specs,
        out_shape=out_shape,
        compiler_params=_params("arbitrary", "arbitrary"),
        name=f"qkv_proj_g{group}",
    )(*operands)


def _attn_kernel(q_ref, kp_ref, kc_ref, vp_ref, vc_ref, o_ref, lse_ref):
    ib = pl.program_id(1)
    tq = q_ref.shape[1]
    blk = ATTN_BLOCK
    scale = HEAD_DIM ** -0.5
    heads = range(ATTN_HEADS)
    row = lax.broadcasted_iota(jnp.int32, (blk, 2 * blk), 0)
    col = lax.broadcasted_iota(jnp.int32, (blk, 2 * blk), 1)
    band = jnp.logical_and(col >= row, col <= row + blk)
    first_band = jnp.logical_and(band, jnp.logical_or(col >= blk, ib > 0))
    lane = lax.broadcasted_iota(jnp.int32, (blk, LANES), 1)
    contract_last = (((1,), (1,)), ((), ()))

    def window(prev_ref, cur_ref, sub, hc):
        if sub == 0:
            return jnp.concatenate([prev_ref[0, :, hc], cur_ref[0, 0:blk, hc]], axis=0)
        return cur_ref[0, (sub - 1) * blk:(sub + 1) * blk, hc]

    chains = [(sub, h) for sub in range(tq // blk) for h in range(ATTN_HEADS)]
    lse_rows = {}
    for c0 in range(0, len(chains), ATTN_CHAIN_BATCH):
        batch = chains[c0:c0 + ATTN_CHAIN_BATCH]
        rows = [slice(sub * blk, (sub + 1) * blk) for sub, _ in batch]
        hcs = [slice(h * HEAD_DIM, (h + 1) * HEAD_DIM) for _, h in batch]
        s = [lax.dot_general(q_ref[0, r, hc], window(kp_ref, kc_ref, sub, hc), contract_last,
                             preferred_element_type=F32)
             for (sub, _), r, hc in zip(batch, rows, hcs)]
        s = [jnp.where(first_band if sub == 0 else band, sh * scale, NEG_INF)
             for (sub, _), sh in zip(batch, s)]
        m = [jnp.max(sh, axis=-1, keepdims=True) for sh in s]
        p = [jnp.exp(sh - mh) for sh, mh in zip(s, m)]
        den = [jnp.sum(ph, axis=-1, keepdims=True) for ph in p]
        o = [_dot(ph.astype(BF16), window(vp_ref, vc_ref, sub, hc))
             for (sub, _), ph, hc in zip(batch, p, hcs)]
        for k, (sub, h) in enumerate(batch):
            o_ref[0, rows[k], hcs[k]] = (o[k] / den[k]).astype(o_ref.dtype)
            acc = lse_rows.get(sub, jnp.zeros((blk, LANES), F32))
            lse_rows[sub] = jnp.where(lane == h, m[k] + jnp.log(den[k]), acc)
    for sub, v in lse_rows.items():
        lse_ref[0, sub * blk:(sub + 1) * blk, :] = v


def _attention(qkv, dilation):
    d, length, n3 = qkv.shape
    n = n3 // 3
    tq = ATTN_Q_ROWS
    sub_per_tile = tq // ATTN_BLOCK
    prev_map = lambda col: (lambda r, i: (r, jnp.maximum(i * sub_per_tile - 1, 0), col))
    cur_map = lambda col: (lambda r, i: (r, i, col))
    return pl.pallas_call(
        _attn_kernel,
        grid=(d, length // tq),
        in_specs=[
            pl.BlockSpec((1, tq, n), cur_map(0)),
            pl.BlockSpec((1, ATTN_BLOCK, n), prev_map(1)),
            pl.BlockSpec((1, tq, n), cur_map(1)),
            pl.BlockSpec((1, ATTN_BLOCK, n), prev_map(2)),
            pl.BlockSpec((1, tq, n), cur_map(2)),
        ],
        out_specs=[
            pl.BlockSpec((1, tq, n), cur_map(0)),
            pl.BlockSpec((1, tq, LANES), cur_map(0)),
        ],
        out_shape=[
            jax.ShapeDtypeStruct((d, length, n), BF16),
            jax.ShapeDtypeStruct((d, length, LANES), F32),
        ],
        compiler_params=_params("parallel", "parallel"),
        name=f"window_attn_d{dilation}",
    )(qkv, qkv, qkv, qkv, qkv)


def _split3(a):
    hi = a.astype(BF16)
    r1 = a - hi.astype(F32)
    mid = r1.astype(BF16)
    lo = (r1 - mid.astype(F32)).astype(BF16)
    return hi, mid, lo


def _merge_kernel(x_ref, o1_ref, o4_ref, o16_ref, l1_ref, l4_ref, l16_ref,
                  p4t_ref, p16t_ref, wout_ref, out_ref):
    t, n = o1_ref.shape
    pr = PERM_ROWS
    subs = range(t // pr)
    heads = range(ATTN_HEADS)

    def regrouped_rows(ref, dil, sub):
        per = pr // dil
        return ref[:, sub * per:(sub + 1) * per, :].reshape(pr, ref.shape[2])

    outs, lses = [], []
    for sub in subs:
        rows = slice(sub * pr, (sub + 1) * pr)
        o_sub, l_sub = [o1_ref[rows, :].astype(F32)], [l1_ref[rows, :]]
        for o_ref, l_ref, pt_ref, dil in ((o4_ref, l4_ref, p4t_ref, ATTN_PATTERNS[1][1]),
                                          (o16_ref, l16_ref, p16t_ref, ATTN_PATTERNS[2][1])):
            pt = pt_ref[...]
            o_sub.append(_dot(pt, regrouped_rows(o_ref, dil, sub)))
            hi, mid, lo = _split3(regrouped_rows(l_ref, dil, sub))
            l_sub.append((_dot(pt, hi) + _dot(pt, mid)) + _dot(pt, lo))
        outs.append(o_sub)
        lses.append(l_sub)
    ws = []
    for l_sub in lses:
        m = jnp.maximum(jnp.maximum(l_sub[0], l_sub[1]), l_sub[2])
        es = [jnp.exp(l - m) for l in l_sub]
        inv = 1.0 / (es[0] + es[1] + es[2])
        ws.append([e * inv for e in es])
    merged = []
    for o_sub, w_sub in zip(outs, ws):
        cols = []
        for h in heads:
            hc = slice(h * HEAD_DIM, (h + 1) * HEAD_DIM)
            acc = w_sub[0][:, h:h + 1] * o_sub[0][:, hc]
            acc += w_sub[1][:, h:h + 1] * o_sub[1][:, hc]
            acc += w_sub[2][:, h:h + 1] * o_sub[2][:, hc]
            cols.append(acc)
        merged.append(jnp.concatenate(cols, axis=1).astype(BF16))
    out_ref[...] = x_ref[...] + _dot(jnp.concatenate(merged, axis=0), wout_ref[...])


def _attn_merge(x, outs, lses, perms_t, w_out):
    s, d = x.shape
    n = w_out.shape[0]
    t = MERGE_ROWS
    d4, d16 = ATTN_PATTERNS[1][1], ATTN_PATTERNS[2][1]
    grouped = lambda dil, width: pl.BlockSpec((dil, t // dil, width), lambda i: (0, i, 0))
    return pl.pallas_call(
        _merge_kernel,
        grid=(s // t,),
        in_specs=[
            pl.BlockSpec((t, d), lambda i: (i, 0)),
            pl.BlockSpec((t, n), lambda i: (i, 0)),
            grouped(d4, n),
            grouped(d16, n),
            pl.BlockSpec((t, LANES), lambda i: (i, 0)),
            grouped(d4, LANES),
            grouped(d16, LANES),
            _resident((PERM_ROWS, PERM_ROWS), lambda i: (0, 0)),
            _resident((PERM_ROWS, PERM_ROWS), lambda i: (0, 0)),
            _resident((n, d), lambda i: (0, 0)),
        ],
        out_specs=pl.BlockSpec((t, d), lambda i: (i, 0)),
        out_shape=jax.ShapeDtypeStruct((s, d), F32),
        compiler_params=_params("parallel"),
        name="attn_merge",
    )(x, outs[0], outs[1], outs[2], lses[0], lses[1], lses[2],
      perms_t[0], perms_t[1], w_out)


def _attn_mixer(x, gain, w_qkv, w_out, attn_layer, next_pool):
    s, d = x.shape
    n = ATTN_HEADS * HEAD_DIM
    perms = [_residue_permutation(PERM_ROWS, dil) for _, dil in ATTN_PATTERNS[1:]]
    p = [jnp.asarray(m, BF16) for m in perms]
    pt = [jnp.asarray(m.T, BF16) for m in perms]
    xn_groups = _norm_perm(x, gain, p)
    hosted = [None if next_pool is None else (next_pool[0], next_pool[2]),
              None if next_pool is None else (next_pool[1], next_pool[2]),
              (w_out, attn_layer)]
    outs, lses, cast = [], [], []
    for g, (_, dil) in enumerate(ATTN_PATTERNS):
        qkv, *hosted_bf = _qkv_proj(xn_groups[g].reshape(s, d), w_qkv, attn_layer, g, hosted[g])
        cast += hosted_bf
        o, lse = _attention(qkv.reshape(dil, s // dil, 3 * n), dil)
        if dil == 1:
            o, lse = o.reshape(s, n), lse.reshape(s, LANES)
        outs.append(o)
        lses.append(lse)
    h = _attn_merge(x, outs, lses, pt, cast[-1])
    return h, (tuple(cast[:2]) if next_pool is not None else None)


def kernel(x, norm_mix, pool_w_in, pool_w_group, pool_scale, pool_w_out, sgu_w_in, sgu_v_norm, sgu_w_s, sgu_b_s, sgu_w_out, attn_w_qkv, attn_w_out, norm_mlp, mlp_w_up, mlp_w_down, norm_final):
    batch, seq, d = x.shape
    depth = norm_mix.shape[0]
    n_mixers = 3
    bf = lambda w: w.astype(BF16)
    row = lambda v: v.reshape(1, -1)
    outs = []
    seqs = [x.reshape(seq, d)] if batch == 1 else [x[b] for b in range(batch)]
    for h in seqs:
        pool_bf = None
        for i in range(depth):
            kind, j = i % n_mixers, i // n_mixers
            gain = row(norm_mix[i])
            if kind == 0:
                first = (mlp_w_up, mlp_w_down, 0) if i == 0 else None
                p_in, p_out = pool_bf if pool_bf else (bf(pool_w_in[j]), bf(pool_w_out[j]))
                pool_bf = None
                h, *first_bf = _pool_mixer(h, gain, p_in, bf(pool_w_group[j]),
                                           row(pool_scale[j]), p_out, first)
                if i == 0:
                    w_up, w_down = first_bf
            elif kind == 1:
                h = _sgu_mixer(h, gain, bf(sgu_w_in[j]), row(sgu_v_norm[j]), bf(sgu_w_s[j]),
                               sgu_b_s[j].T, bf(sgu_w_out[j]))
            else:
                pool_next = i + 1 < depth and (i + 1) % n_mixers == 0
                next_pool = (pool_w_in, pool_w_out, (i + 1) // n_mixers) if pool_next else None
                h, pool_bf = _attn_mixer(h, gain, attn_w_qkv, attn_w_out, j, next_pool)
            nxt = (mlp_w_up, mlp_w_down, i + 1) if i + 1 < depth else None
            h, *next_bf = _mlp(h, row(norm_mlp[i]), w_up, w_down, row(norm_final), nxt)
            if nxt is not None:
                w_up, w_down = next_bf
        outs.append(h)
    return outs[0].reshape(1, seq, d) if batch == 1 else jnp.stack(outs, axis=0)
```

```python
import functools

import jax
import jax.numpy as jnp
import numpy as np
from jax import lax
from jax.experimental import pallas as pl
from jax.experimental.pallas import tpu as pltpu

F32 = jnp.float32
BF16 = jnp.bfloat16

RMS_EPS = 1e-6
POOL_WINDOWS = (2, 4, 8, 16)
SGU_CHUNK = 128
SGU_GROUPS = 8
ATTN_PATTERNS = ((128, 1), (512, 4), (2048, 16))
ATTN_HEADS = 8
HEAD_DIM = 128
ATTN_BLOCK = 128
NEG_INF = -1e30

V7X_VMEM_LIMIT_BYTES = 62 * 1024 * 1024
LANES = 128

MLP_ROWS = 1024
MLP_FF_COLS = 1024
MLP_FIRST_STEP_CHUNKS = 4
PROLOGUE_ROW_CHUNKS = 2
MIX_ROWS = 512
SGU_ROWS = 512
PERM_ROWS = 256
MERGE_ROWS = 512
NORM_PERM_ROWS = 1024
PROJ_ROWS = 2048
ATTN_Q_ROWS = 1024
ATTN_CHAIN_BATCH = 8
POOL_HALO = 32


def _params(*semantics):
    return pltpu.CompilerParams(dimension_semantics=semantics,
                                vmem_limit_bytes=V7X_VMEM_LIMIT_BYTES)


def _resident(shape, index_map):
    return pl.BlockSpec(shape, index_map, pipeline_mode=pl.Buffered(1))


def _rmsnorm(xf, gain_row):
    ms = jnp.mean(xf * xf, axis=-1, keepdims=True)
    return (xf * lax.rsqrt(ms + RMS_EPS)) * gain_row


def _dot(a, b):
    return lax.dot_general(a, b, (((1,), (0,)), ((), ())), preferred_element_type=F32)


def _mlp_weight_cast_specs(w_up, w_down, layer, steps, step_of):
    _, d, ff = w_up.shape
    in_specs = [pl.BlockSpec((None, d // steps, ff), lambda *g: (layer, step_of(*g), 0)),
                pl.BlockSpec((None, ff // steps, d), lambda *g: (layer, step_of(*g), 0))]
    out_specs = [pl.BlockSpec((d // steps, ff), lambda *g: (step_of(*g), 0)),
                 pl.BlockSpec((ff // steps, d), lambda *g: (step_of(*g), 0))]
    out_shape = [jax.ShapeDtypeStruct((d, ff), BF16), jax.ShapeDtypeStruct((ff, d), BF16)]
    return in_specs, out_specs, out_shape


def _cast_mlp_weights(wup_ref, wdn_ref, wup_bf_ref, wdn_bf_ref):
    wup_bf_ref[...] = wup_ref[...].astype(BF16)
    wdn_bf_ref[...] = wdn_ref[...].astype(BF16)


def _mlp_kernel(x_ref, g_ref, wup_ref, wdn_ref, gf_ref, *refs, last):
    if last:
        o_ref, xn_ref = refs
    else:
        nup_ref, ndn_ref, o_ref, nup_bf_ref, ndn_bf_ref, xn_ref = refs
    f = pl.program_id(1)
    chunk = x_ref.shape[0] // MLP_FIRST_STEP_CHUNKS
    row_chunks = [slice(r * chunk, (r + 1) * chunk) for r in range(MLP_FIRST_STEP_CHUNKS)]

    def mlp(xn):
        h = jnp.maximum(_dot(xn, wup_ref[...]), 0.0)
        return _dot((h * h).astype(BF16), wdn_ref[...])

    def cast_next():
        if not last:
            _cast_mlp_weights(nup_ref, ndn_ref, nup_bf_ref, ndn_bf_ref)

    @pl.when(f == 0)
    def _():
        for rows in row_chunks:
            x = x_ref[rows, :]
            xn = _rmsnorm(x, g_ref[...]).astype(BF16)
            xn_ref[rows, :] = xn
            o_ref[rows, :] = x + mlp(xn)
        cast_next()

    @pl.when(f > 0)
    def _():
        o_ref[...] += mlp(xn_ref[...])
        cast_next()

    if last:
        @pl.when(f == pl.num_programs(1) - 1)
        def _():
            o_ref[...] = _rmsnorm(o_ref[...], gf_ref[...])


def _mlp(x, gain, w_up, w_down, final_gain, next_weights):
    s, d = x.shape
    ff = w_up.shape[1]
    tm, tf = MLP_ROWS, MLP_FF_COLS
    last = next_weights is None
    n_chunks = ff // tf
    in_specs = [
        pl.BlockSpec((tm, d), lambda i, f: (i, 0)),
        pl.BlockSpec((1, d), lambda i, f: (0, 0)),
        pl.BlockSpec((d, tf), lambda i, f: (0, f)),
        pl.BlockSpec((tf, d), lambda i, f: (f, 0)),
        pl.BlockSpec((1, d), lambda i, f: (0, 0)),
    ]
    out_specs = [pl.BlockSpec((tm, d), lambda i, f: (i, 0))]
    out_shape = [jax.ShapeDtypeStruct((s, d), F32)]
    operands = [x, gain, w_up, w_down, final_gain]
    if not last:
        cast_in, cast_out, cast_shape = _mlp_weight_cast_specs(
            *next_weights, (s // tm) * n_chunks, lambda i, f: i * n_chunks + f)
        in_specs += cast_in
        out_specs += cast_out
        out_shape += cast_shape
        operands += list(next_weights[:2])
    return pl.pallas_call(
        functools.partial(_mlp_kernel, last=last),
        grid=(s // tm, n_chunks),
        in_specs=in_specs,
        out_specs=out_specs,
        out_shape=out_shape,
        scratch_shapes=[pltpu.VMEM((tm, d), BF16)],
        compiler_params=_params("parallel", "arbitrary"),
        name="mlp_final" if last else "mlp",
    )(*operands)


def _pool_kernel(x_ref, g_ref, win_ref, wg_ref, sc_ref, wout_ref, *refs, cast):
    if cast:
        wup_ref, wdn_ref, o_ref, wup_bf_ref, wdn_bf_ref, hext_ref, t0_ref, t1_ref = refs
    else:
        o_ref, hext_ref, t0_ref, t1_ref = refs
    b = pl.program_id(0)
    tm, d = x_ref.shape
    halo = POOL_HALO
    gc = d // len(POOL_WINDOWS)

    @pl.when(b == 0)
    def _():
        hext_ref[0:halo, :] = jnp.zeros((halo, d), F32)
        t0_ref[0:halo, :] = jnp.zeros((halo, gc), F32)
        t1_ref[0:halo, :] = jnp.zeros((halo, gc), F32)

    @pl.when(b > 0)
    def _():
        hext_ref[halo - 16:halo, :] = hext_ref[tm + halo - 16:tm + halo, :]

    chunk = tm // PROLOGUE_ROW_CHUNKS
    for r in range(PROLOGUE_ROW_CHUNKS):
        xn = _rmsnorm(x_ref[r * chunk:(r + 1) * chunk, :], g_ref[...]).astype(BF16)
        hext_ref[halo + r * chunk:halo + (r + 1) * chunk, :] = _dot(xn, win_ref[...])
    x = x_ref[...]

    pos = b * tm + lax.broadcasted_iota(jnp.int32, (tm, 1), 0)
    temps = (t0_ref, t1_ref)
    mixed = []
    for g, w in enumerate(POOL_WINDOWS):
        cols = slice(g * gc, (g + 1) * gc)
        src, src_cols, shift, levels = hext_ref, cols, 1, g + 1
        for lvl in range(levels):
            last = lvl == levels - 1
            lo = halo if last else halo - 16
            n = tm if last else tm + 16
            s = src[lo:lo + n, src_cols] + src[lo - shift:lo - shift + n, src_cols]
            if not last:
                dst = temps[lvl % 2]
                dst[lo:lo + n, :] = s
                src, src_cols = dst, slice(None)
            shift *= 2
        inv_count = 1.0 / jnp.minimum(pos + 1, w).astype(F32)
        pooled = s * inv_count - hext_ref[halo:halo + tm, cols]
        mixed.append(_dot(pooled.astype(BF16), wg_ref[g]))
    mixed = (jnp.concatenate(mixed, axis=1) * sc_ref[...]).astype(BF16)
    o_ref[...] = x + _dot(mixed, wout_ref[...])
    if cast:
        _cast_mlp_weights(wup_ref, wdn_ref, wup_bf_ref, wdn_bf_ref)


def _pool_mixer(x, gain, w_in, w_group, scale, w_out, cast_weights):
    s, d = x.shape
    tm = MIX_ROWS
    ng, gc, _ = w_group.shape
    steps = s // tm
    in_specs = [
        pl.BlockSpec((tm, d), lambda i: (i, 0)),
        _resident((1, d), lambda i: (0, 0)),
        _resident((d, d), lambda i: (0, 0)),
        _resident((ng, gc, gc), lambda i: (0, 0, 0)),
        _resident((1, d), lambda i: (0, 0)),
        _resident((d, d), lambda i: (0, 0)),
    ]
    out_specs = [pl.BlockSpec((tm, d), lambda i: (i, 0))]
    out_shape = [jax.ShapeDtypeStruct((s, d), F32)]
    operands = [x, gain, w_in, w_group, scale, w_out]
    cast = cast_weights is not None
    if cast:
        cast_in, cast_out, cast_shape = _mlp_weight_cast_specs(*cast_weights, steps, lambda i: i)
        in_specs += cast_in
        out_specs += cast_out
        out_shape += cast_shape
        operands += list(cast_weights[:2])
    return pl.pallas_call(
        functools.partial(_pool_kernel, cast=cast),
        grid=(steps,),
        in_specs=in_specs,
        out_specs=out_specs,
        out_shape=out_shape,
        scratch_shapes=[pltpu.VMEM((tm + POOL_HALO, d), F32),
                        pltpu.VMEM((tm + POOL_HALO, gc), F32),
                        pltpu.VMEM((tm + POOL_HALO, gc), F32)],
        compiler_params=_params("arbitrary"),
        name="pool_mixer",
    )(*operands)


def _gelu(x):
    return 0.5 * x * (1.0 + lax.erf(x * np.float32(np.sqrt(0.5))))


def _sgu_kernel(x_ref, g_ref, win_ref, vn_ref, ws_ref, bs_ref, wout_ref, o_ref,
                u_ref, v_ref, gated_ref):
    tm, d = x_ref.shape
    e = win_ref.shape[1] // 2
    gcols = e // SGU_GROUPS
    chunk = tm // PROLOGUE_ROW_CHUNKS
    for r in range(PROLOGUE_ROW_CHUNKS):
        rows = slice(r * chunk, (r + 1) * chunk)
        xn = _rmsnorm(x_ref[rows, :], g_ref[...]).astype(BF16)
        u_ref[rows, :] = _gelu(_dot(xn, win_ref[:, :e]))
        v = _gelu(_dot(xn, win_ref[:, e:]))
        v_ref[rows, :] = _rmsnorm(v, vn_ref[...]).astype(BF16)
    x = x_ref[...]

    t = lax.broadcasted_iota(jnp.int32, (SGU_CHUNK, SGU_CHUNK), 0)
    sidx = lax.broadcasted_iota(jnp.int32, (SGU_CHUNK, SGU_CHUNK), 1)
    causal = sidx <= t
    for g in range(SGU_GROUPS):
        ws = jnp.where(causal, ws_ref[g], jnp.zeros_like(ws_ref[g]))
        bias = bs_ref[:, g:g + 1]
        cols = slice(g * gcols, (g + 1) * gcols)
        for c in range(tm // SGU_CHUNK):
            rows = slice(c * SGU_CHUNK, (c + 1) * SGU_CHUNK)
            sp = _dot(ws, v_ref[rows, cols]) + bias
            gated_ref[rows, cols] = (u_ref[rows, cols] * sp).astype(BF16)
    o_ref[...] = x + _dot(gated_ref[...], wout_ref[...])


def _sgu_mixer(x, gain, w_in, v_norm, w_s, b_s_t, w_out):
    s, d = x.shape
    tm = SGU_ROWS
    e = w_out.shape[0]
    return pl.pallas_call(
        _sgu_kernel,
        grid=(s // tm,),
        in_specs=[
            pl.BlockSpec((tm, d), lambda i: (i, 0)),
            _resident((1, d), lambda i: (0, 0)),
            _resident((d, 2 * e), lambda i: (0, 0)),
            _resident((1, e), lambda i: (0, 0)),
            _resident(w_s.shape, lambda i: (0, 0, 0)),
            _resident(b_s_t.shape, lambda i: (0, 0)),
            _resident((e, d), lambda i: (0, 0)),
        ],
        out_specs=pl.BlockSpec((tm, d), lambda i: (i, 0)),
        out_shape=jax.ShapeDtypeStruct((s, d), F32),
        scratch_shapes=[pltpu.VMEM((tm, e), F32),
                        pltpu.VMEM((tm, e), BF16),
                        pltpu.VMEM((tm, e), BF16)],
        compiler_params=_params("parallel"),
        name="sgu_mixer",
    )(x, gain, w_in, v_norm, w_s, b_s_t, w_out)


def _residue_permutation(rows, dilation):
    n = np.arange(rows)
    per = rows // dilation
    old = dilation * (n % per) + n // per
    p = np.zeros((rows, rows), np.float32)
    p[n, old] = 1.0
    return p


def _norm_perm_kernel(x_ref, g_ref, p4_ref, p16_ref, o1_ref, o4_ref, o16_ref):
    t = x_ref.shape[0]
    pr = PERM_ROWS
    xn = _rmsnorm(x_ref[...], g_ref[...]).astype(BF16)
    o1_ref[...] = xn
    for p_ref, o_ref in ((p4_ref, o4_ref), (p16_ref, o16_ref)):
        dil = o_ref.shape[0]
        per = pr // dil
        for sub in range(t // pr):
            y = _dot(p_ref[...], xn[sub * pr:(sub + 1) * pr, :]).astype(BF16)
            o_ref[:, sub * per:(sub + 1) * per, :] = y.reshape(dil, per, y.shape[1])


def _norm_perm(x, gain, perms):
    s, d = x.shape
    t = NORM_PERM_ROWS
    d4, d16 = ATTN_PATTERNS[1][1], ATTN_PATTERNS[2][1]
    return pl.pallas_call(
        _norm_perm_kernel,
        grid=(s // t,),
        in_specs=[
            pl.BlockSpec((t, d), lambda i: (i, 0)),
            _resident((1, d), lambda i: (0, 0)),
            _resident((PERM_ROWS, PERM_ROWS), lambda i: (0, 0)),
            _resident((PERM_ROWS, PERM_ROWS), lambda i: (0, 0)),
        ],
        out_specs=[
            pl.BlockSpec((t, d), lambda i: (i, 0)),
            pl.BlockSpec((d4, t // d4, d), lambda i: (0, i, 0)),
            pl.BlockSpec((d16, t // d16, d), lambda i: (0, i, 0)),
        ],
        out_shape=[
            jax.ShapeDtypeStruct((s, d), BF16),
            jax.ShapeDtypeStruct((d4, s // d4, d), BF16),
            jax.ShapeDtypeStruct((d16, s // d16, d), BF16),
        ],
        compiler_params=_params("parallel"),
        name="attn_norm_perm",
    )(x, gain, perms[0], perms[1])


def _proj_kernel(a_ref, w_ref, *refs, cast):
    if cast:
        r_ref, o_ref, r_bf_ref = refs

        @pl.when(pl.program_id(0) == 0)
        def _():
            r_bf_ref[...] = r_ref[...].astype(BF16)
    else:
        (o_ref,) = refs
    o_ref[...] = _dot(a_ref[...], w_ref[...]).astype(o_ref.dtype)


def _qkv_proj(a, w_qkv, layer, group, hosted):
    s, d = a.shape
    n = ATTN_HEADS * HEAD_DIM
    n_groups = len(ATTN_PATTERNS)
    tm = PROJ_ROWS
    blocks = s // tm
    in_specs = [
        pl.BlockSpec((tm, d), lambda j, i: (i, 0)),
        pl.BlockSpec((None, d, n), lambda j, i: (layer, 0, j * n_groups + group)),
    ]
    out_specs = [pl.BlockSpec((tm, n), lambda j, i: (i, j))]
    out_shape = [jax.ShapeDtypeStruct((s, 3 * n), BF16)]
    operands = [a, w_qkv]
    if hosted is not None:
        stack, hosted_layer = hosted
        _, r, c = stack.shape
        blk = lambda j, i: jnp.where(j == 0, i, blocks - 1)
        in_specs.append(pl.BlockSpec((None, r // blocks, c),
                                     lambda j, i: (hosted_layer, blk(j, i), 0)))
        out_specs.append(pl.BlockSpec((r // blocks, c), lambda j, i: (blk(j, i), 0)))
        out_shape.append(jax.ShapeDtypeStruct((r, c), BF16))
        operands.append(stack)
    return pl.pallas_call(
        functools.partial(_proj_kernel, cast=hosted is not None),
        grid=(3, blocks),
        in_specs=in_specs,
        out_specs=out_specs,
        out_shape=out_shape,
        compiler_params=_params("arbitrary", "arbitrary"),
        name=f"qkv_proj_g{group}",
    )(*operands)


def _attn_kernel(q_ref, kp_ref, kc_ref, vp_ref, vc_ref, o_ref, lse_ref):
    ib = pl.program_id(1)
    tq = q_ref.shape[1]
    blk = ATTN_BLOCK
    scale = HEAD_DIM ** -0.5
    heads = range(ATTN_HEADS)
    row = lax.broadcasted_iota(jnp.int32, (blk, 2 * blk), 0)
    col = lax.broadcasted_iota(jnp.int32, (blk, 2 * blk), 1)
    band = jnp.logical_and(col >= row, col <= row + blk)
    first_band = jnp.logical_and(band, jnp.logical_or(col >= blk, ib > 0))
    lane = lax.broadcasted_iota(jnp.int32, (blk, LANES), 1)
    contract_last = (((1,), (1,)), ((), ()))

    def window(prev_ref, cur_ref, sub, hc):
        if sub == 0:
            return jnp.concatenate([prev_ref[0, :, hc], cur_ref[0, 0:blk, hc]], axis=0)
        return cur_ref[0, (sub - 1) * blk:(sub + 1) * blk, hc]

    chains = [(sub, h) for sub in range(tq // blk) for h in range(ATTN_HEADS)]
    lse_rows = {}
    for c0 in range(0, len(chains), ATTN_CHAIN_BATCH):
        batch = chains[c0:c0 + ATTN_CHAIN_BATCH]
        rows = [slice(sub * blk, (sub + 1) * blk) for sub, _ in batch]
        hcs = [slice(h * HEAD_DIM, (h + 1) * HEAD_DIM) for _, h in batch]
        s = [lax.dot_general(q_ref[0, r, hc], window(kp_ref, kc_ref, sub, hc), contract_last,
                             preferred_element_type=F32)
             for (sub, _), r, hc in zip(batch, rows, hcs)]
        s = [jnp.where(first_band if sub == 0 else band, sh * scale, NEG_INF)
             for (sub, _), sh in zip(batch, s)]
        m = [jnp.max(sh, axis=-1, keepdims=True) for sh in s]
        p = [jnp.exp(sh - mh) for sh, mh in zip(s, m)]
        den = [jnp.sum(ph, axis=-1, keepdims=True) for ph in p]
        o = [_dot(ph.astype(BF16), window(vp_ref, vc_ref, sub, hc))
             for (sub, _), ph, hc in zip(batch, p, hcs)]
        for k, (sub, h) in enumerate(batch):
            o_ref[0, rows[k], hcs[k]] = (o[k] / den[k]).astype(o_ref.dtype)
            acc = lse_rows.get(sub, jnp.zeros((blk, LANES), F32))
            lse_rows[sub] = jnp.where(lane == h, m[k] + jnp.log(den[k]), acc)
    for sub, v in lse_rows.items():
        lse_ref[0, sub * blk:(sub + 1) * blk, :] = v


def _attention(qkv, dilation):
    d, length, n3 = qkv.shape
    n = n3 // 3
    tq = ATTN_Q_ROWS
    sub_per_tile = tq // ATTN_BLOCK
    prev_map = lambda col: (lambda r, i: (r, jnp.maximum(i * sub_per_tile - 1, 0), col))
    cur_map = lambda col: (lambda r, i: (r, i, col))
    return pl.pallas_call(
        _attn_kernel,
        grid=(d, length // tq),
        in_specs=[
            pl.BlockSpec((1, tq, n), cur_map(0)),
            pl.BlockSpec((1, ATTN_BLOCK, n), prev_map(1)),
            pl.BlockSpec((1, tq, n), cur_map(1)),
            pl.BlockSpec((1, ATTN_BLOCK, n), prev_map(2)),
            pl.BlockSpec((1, tq, n), cur_map(2)),
        ],
        out_specs=[
            pl.BlockSpec((1, tq, n), cur_map(0)),
            pl.BlockSpec((1, tq, LANES), cur_map(0)),
        ],
        out_shape=[
            jax.ShapeDtypeStruct((d, length, n), BF16),
            jax.ShapeDtypeStruct((d, length, LANES), F32),
        ],
        compiler_params=_params("parallel", "parallel"),
        name=f"window_attn_d{dilation}",
    )(qkv, qkv, qkv, qkv, qkv)


def _split3(a):
    hi = a.astype(BF16)
    r1 = a - hi.astype(F32)
    mid = r1.astype(BF16)
    lo = (r1 - mid.astype(F32)).astype(BF16)
    return hi, mid, lo


def _merge_kernel(x_ref, o1_ref, o4_ref, o16_ref, l1_ref, l4_ref, l16_ref,
                  p4t_ref, p16t_ref, wout_ref, out_ref):
    t, n = o1_ref.shape
    pr = PERM_ROWS
    subs = range(t // pr)
    heads = range(ATTN_HEADS)

    def regrouped_rows(ref, dil, sub):
        per = pr // dil
        return ref[:, sub * per:(sub + 1) * per, :].reshape(pr, ref.shape[2])

    outs, lses = [], []
    for sub in subs:
        rows = slice(sub * pr, (sub + 1) * pr)
        o_sub, l_sub = [o1_ref[rows, :].astype(F32)], [l1_ref[rows, :]]
        for o_ref, l_ref, pt_ref, dil in ((o4_ref, l4_ref, p4t_ref, ATTN_PATTERNS[1][1]),
                                          (o16_ref, l16_ref, p16t_ref, ATTN_PATTERNS[2][1])):
            pt = pt_ref[...]
            o_sub.append(_dot(pt, regrouped_rows(o_ref, dil, sub)))
            hi, mid, lo = _split3(regrouped_rows(l_ref, dil, sub))
            l_sub.append((_dot(pt, hi) + _dot(pt, mid)) + _dot(pt, lo))
        outs.append(o_sub)
        lses.append(l_sub)
    ws = []
    for l_sub in lses:
        m = jnp.maximum(jnp.maximum(l_sub[0], l_sub[1]), l_sub[2])
        es = [jnp.exp(l - m) for l in l_sub]
        inv = 1.0 / (es[0] + es[1] + es[2])
        ws.append([e * inv for e in es])
    merged = []
    for o_sub, w_sub in zip(outs, ws):
        cols = []
        for h in heads:
            hc = slice(h * HEAD_DIM, (h + 1) * HEAD_DIM)
            acc = w_sub[0][:, h:h + 1] * o_sub[0][:, hc]
            acc += w_sub[1][:, h:h + 1] * o_sub[1][:, hc]
            acc += w_sub[2][:, h:h + 1] * o_sub[2][:, hc]
            cols.append(acc)
        merged.append(jnp.concatenate(cols, axis=1).astype(BF16))
    out_ref[...] = x_ref[...] + _dot(jnp.concatenate(merged, axis=0), wout_ref[...])


def _attn_merge(x, outs, lses, perms_t, w_out):
    s, d = x.shape
    n = w_out.shape[0]
    t = MERGE_ROWS
    d4, d16 = ATTN_PATTERNS[1][1], ATTN_PATTERNS[2][1]
    grouped = lambda dil, width: pl.BlockSpec((dil, t // dil, width), lambda i: (0, i, 0))
    return pl.pallas_call(
        _merge_kernel,
        grid=(s // t,),
        in_specs=[
            pl.BlockSpec((t, d), lambda i: (i, 0)),
            pl.BlockSpec((t, n), lambda i: (i, 0)),
            grouped(d4, n),
            grouped(d16, n),
            pl.BlockSpec((t, LANES), lambda i: (i, 0)),
            grouped(d4, LANES),
            grouped(d16, LANES),
            _resident((PERM_ROWS, PERM_ROWS), lambda i: (0, 0)),
            _resident((PERM_ROWS, PERM_ROWS), lambda i: (0, 0)),
            _resident((n, d), lambda i: (0, 0)),
        ],
        out_specs=pl.BlockSpec((t, d), lambda i: (i, 0)),
        out_shape=jax.ShapeDtypeStruct((s, d), F32),
        compiler_params=_params("parallel"),
        name="attn_merge",
    )(x, outs[0], outs[1], outs[2], lses[0], lses[1], lses[2],
      perms_t[0], perms_t[1], w_out)


def _attn_mixer(x, gain, w_qkv, w_out, attn_layer, next_pool):
    s, d = x.shape
    n = ATTN_HEADS * HEAD_DIM
    perms = [_residue_permutation(PERM_ROWS, dil) for _, dil in ATTN_PATTERNS[1:]]
    p = [jnp.asarray(m, BF16) for m in perms]
    pt = [jnp.asarray(m.T, BF16) for m in perms]
    xn_groups = _norm_perm(x, gain, p)
    hosted = [None if next_pool is None else (next_pool[0], next_pool[2]),
              None if next_pool is None else (next_pool[1], next_pool[2]),
              (w_out, attn_layer)]
    outs, lses, cast = [], [], []
    for g, (_, dil) in enumerate(ATTN_PATTERNS):
        qkv, *hosted_bf = _qkv_proj(xn_groups[g].reshape(s, d), w_qkv, attn_layer, g, hosted[g])
        cast += hosted_bf
        o, lse = _attention(qkv.reshape(dil, s // dil, 3 * n), dil)
        if dil == 1:
            o, lse = o.reshape(s, n), lse.reshape(s, LANES)
        outs.append(o)
        lses.append(lse)
    h = _attn_merge(x, outs, lses, pt, cast[-1])
    return h, (tuple(cast[:2]) if next_pool is not None else None)


def kernel(x, norm_mix, pool_w_in, pool_w_group, pool_scale, pool_w_out, sgu_w_in, sgu_v_norm, sgu_w_s, sgu_b_s, sgu_w_out, attn_w_qkv, attn_w_out, norm_mlp, mlp_w_up, mlp_w_down, norm_final):
    batch, seq, d = x.shape
    depth = norm_mix.shape[0]
    n_mixers = 3
    bf = lambda w: w.astype(BF16)
    row = lambda v: v.reshape(1, -1)
    outs = []
    seqs = [x.reshape(seq, d)] if batch == 1 else [x[b] for b in range(batch)]
    for h in seqs:
        pool_bf = None
        for i in range(depth):
            kind, j = i % n_mixers, i // n_mixers
            gain = row(norm_mix[i])
            if kind == 0:
                first = (mlp_w_up, mlp_w_down, 0) if i == 0 else None
                p_in, p_out = pool_bf if pool_bf else (bf(pool_w_in[j]), bf(pool_w_out[j]))
                pool_bf = None
                h, *first_bf = _pool_mixer(h, gain, p_in, bf(pool_w_group[j]),
                                           row(pool_scale[j]), p_out, first)
                if i == 0:
                    w_up, w_down = first_bf
            elif kind == 1:
                h = _sgu_mixer(h, gain, bf(sgu_w_in[j]), row(sgu_v_norm[j]), bf(sgu_w_s[j]),
                               sgu_b_s[j].T, bf(sgu_w_out[j]))
            else:
                pool_next = i + 1 < depth and (i + 1) % n_mixers == 0
                next_pool = (pool_w_in, pool_w_out, (i + 1) // n_mixers) if pool_next else None
                h, pool_bf = _attn_mixer(h, gain, attn_w_qkv, attn_w_out, j, next_pool)
            nxt = (mlp_w_up, mlp_w_down, i + 1) if i + 1 < depth else None
            h, *next_bf = _mlp(h, row(norm_mlp[i]), w_up, w_down, row(norm_final), nxt)
            if nxt is not None:
                w_up, w_down = next_bf
        outs.append(h)
    return outs[0].reshape(1, seq, d) if batch == 1 else jnp.stack(outs, axis=0)
```
